```python
import jax, jax.numpy as jnp
from jax import lax
import numpy as np

D_MODEL = 1024
BATCH = 8
SEQ = 4096
DEPTH = 4
DEC_BATCH = 16
DEC_SEQ = 4096
PAST_LEN = 128

N_MIXERS = 2
N_ATTN_LAYERS = (DEPTH + 1) // 2
N_CONV_LAYERS = DEPTH // 2
HEAD_DIM = 128
N_HEADS = D_MODEL // HEAD_DIM
N_KV_HEADS = 2
GQA_GROUP = N_HEADS // N_KV_HEADS
ATTN_WIDTH = N_HEADS * HEAD_DIM
KV_WIDTH = N_KV_HEADS * HEAD_DIM
ATTN_IN_WIDTH = 2 * ATTN_WIDTH + 2 * KV_WIDTH
ROPE_AXIS_DIM = HEAD_DIM // 2
ROPE_THETA = 10000.0
Q_BLOCK = 128
CONV_WIDTH = D_MODEL
CONV_IN_WIDTH = 3 * CONV_WIDTH
CONV_KERNEL = 31
CONV_PAD = CONV_KERNEL // 2
GRID_W = 64
EPS = 1e-6

kernel_name = "hybrid_axial_gqa_conformer_encoder"


def rms_norm(x, g):
    xf = x.astype(jnp.float32)
    y = xf * lax.rsqrt(jnp.mean(xf * xf, axis=-1, keepdims=True) + EPS)
    return (y * g.astype(jnp.float32)).astype(x.dtype)


def layer_norm(x, g, b):
    xf = x.astype(jnp.float32)
    mu = jnp.mean(xf, axis=-1, keepdims=True)
    xc = xf - mu
    var = jnp.mean(xc * xc, axis=-1, keepdims=True)
    y = xc * lax.rsqrt(var + EPS) * g.astype(jnp.float32) + b.astype(jnp.float32)
    return y.astype(x.dtype)


def axial_rope_tables(seq_len):
    rows = seq_len // GRID_W
    row = jnp.repeat(jnp.arange(rows, dtype=jnp.float32), GRID_W)
    col = jnp.tile(jnp.arange(GRID_W, dtype=jnp.float32), rows)
    inv_freq = ROPE_THETA ** (-jnp.arange(0, ROPE_AXIS_DIM, 2, dtype=jnp.float32) / ROPE_AXIS_DIM)
    ang_r = row[:, None] * inv_freq[None, :]
    ang_c = col[:, None] * inv_freq[None, :]
    ang = jnp.concatenate([ang_r, ang_r, ang_c, ang_c], axis=-1)
    return jnp.cos(ang), jnp.sin(ang)


def _rotate_half(h):
    a, b = jnp.split(h, 2, axis=-1)
    return jnp.concatenate([-b, a], axis=-1)


def apply_axial_rope(x, cos, sin):
    xf = x.astype(jnp.float32)
    xr, xc = jnp.split(xf, 2, axis=-1)
    rot = jnp.concatenate([_rotate_half(xr), _rotate_half(xc)], axis=-1)
    out = xf * cos[None, :, None, :] + rot * sin[None, :, None, :]
    return out.astype(x.dtype)


def attention_mixer(h, w_in, q_g, k_g, w_out, cos, sin):
    B, S, _ = h.shape
    proj = h @ w_in
    q, k, v, z = jnp.split(proj, [ATTN_WIDTH, ATTN_WIDTH + KV_WIDTH, ATTN_WIDTH + 2 * KV_WIDTH], axis=-1)
    q = rms_norm(q.reshape(B, S, N_HEADS, HEAD_DIM), q_g)
    k = rms_norm(k.reshape(B, S, N_KV_HEADS, HEAD_DIM), k_g)
    v = v.reshape(B, S, N_KV_HEADS, HEAD_DIM)
    q = apply_axial_rope(q, cos, sin)
    k = apply_axial_rope(k, cos, sin)
    n_blk = S // Q_BLOCK
    qb = q.reshape(B, n_blk, Q_BLOCK, N_KV_HEADS, GQA_GROUP, HEAD_DIM).transpose(1, 0, 2, 3, 4, 5)
    scale = HEAD_DIM ** -0.5

    def one_block(q_blk):
        s = jnp.einsum('bqkgd,bskd->bkgqs', q_blk, k, preferred_element_type=jnp.float32) * scale
        p = jax.nn.softmax(s, axis=-1).astype(v.dtype)
        return jnp.einsum('bkgqs,bskd->bqkgd', p, v)

    o = lax.map(one_block, qb)
    o = o.transpose(1, 0, 2, 3, 4, 5).reshape(B, S, ATTN_WIDTH)
    o = o * jax.nn.silu(z)
    return o @ w_out


def conv_mixer(h, w_in, dw_w, dw_b, ln_g, ln_b, w_out):
    proj = h @ w_in
    a, g, z = jnp.split(proj, 3, axis=-1)
    u = a * jax.nn.sigmoid(g)
    u = lax.conv_general_dilated(
        u, dw_w[:, None, :], window_strides=(1,), padding=[(CONV_PAD, CONV_PAD)],
        dimension_numbers=('NWC', 'WIO', 'NWC'), feature_group_count=CONV_WIDTH) + dw_b
    u = jax.nn.silu(layer_norm(u, ln_g, ln_b))
    u = u * jax.nn.silu(z)
    return u @ w_out


def trunk(x, pre_norm_g, post_norm_g, attn_w_in, attn_q_norm_g, attn_k_norm_g, attn_w_out,
          conv_w_in, conv_dw_w, conv_dw_b, conv_ln_g, conv_ln_b, conv_w_out):
    cos, sin = axial_rope_tables(x.shape[1])
    for i in range(DEPTH):
        j = i // N_MIXERS
        h = rms_norm(x, pre_norm_g[i])
        if i % N_MIXERS == 0:
            m = attention_mixer(h, attn_w_in[j], attn_q_norm_g[j], attn_k_norm_g[j], attn_w_out[j], cos, sin)
        else:
            m = conv_mixer(h, conv_w_in[j], conv_dw_w[j], conv_dw_b[j], conv_ln_g[j], conv_ln_b[j], conv_w_out[j])
        x = x + rms_norm(m, post_norm_g[i])
    return x


def setup_inputs(seed: int = 0) -> dict:
    key = jax.random.key(seed)
    ks = jax.random.split(key, 16)
    f32 = jnp.float32
    nrm = lambda k, shape, s: jax.random.normal(k, shape, f32) * s
    return {
        "x_prompt": nrm(ks[0], (BATCH, SEQ, D_MODEL), 1.0),
        "x_sample": nrm(ks[1], (DEC_BATCH, DEC_SEQ, D_MODEL), 1.0),
        "pre_norm_g": 1.0 + nrm(ks[2], (DEPTH, D_MODEL), 0.02),
        "post_norm_g": 1.0 + nrm(ks[3], (DEPTH, D_MODEL), 0.02),
        "attn_w_in": nrm(ks[4], (N_ATTN_LAYERS, D_MODEL, ATTN_IN_WIDTH), D_MODEL ** -0.5),
        "attn_q_norm_g": 1.0 + nrm(ks[5], (N_ATTN_LAYERS, HEAD_DIM), 0.02),
        "attn_k_norm_g": 1.0 + nrm(ks[6], (N_ATTN_LAYERS, HEAD_DIM), 0.02),
        "attn_w_out": nrm(ks[7], (N_ATTN_LAYERS, ATTN_WIDTH, D_MODEL), ATTN_WIDTH ** -0.5),
        "conv_w_in": nrm(ks[8], (N_CONV_LAYERS, D_MODEL, CONV_IN_WIDTH), D_MODEL ** -0.5),
        "conv_dw_w": nrm(ks[9], (N_CONV_LAYERS, CONV_KERNEL, CONV_WIDTH), CONV_KERNEL ** -0.5),
        "conv_dw_b": nrm(ks[10], (N_CONV_LAYERS, CONV_WIDTH), 0.02),
        "conv_ln_g": 1.0 + nrm(ks[11], (N_CONV_LAYERS, CONV_WIDTH), 0.02),
        "conv_ln_b": nrm(ks[12], (N_CONV_LAYERS, CONV_WIDTH), 0.02),
        "conv_w_out": nrm(ks[13], (N_CONV_LAYERS, CONV_WIDTH, D_MODEL), CONV_WIDTH ** -0.5),
    }


def reference(x_prompt, x_sample, pre_norm_g, post_norm_g, attn_w_in, attn_q_norm_g, attn_k_norm_g,
              attn_w_out, conv_w_in, conv_dw_w, conv_dw_b, conv_ln_g, conv_ln_b, conv_w_out):
    y_prompt = trunk(x_prompt, pre_norm_g, post_norm_g, attn_w_in, attn_q_norm_g, attn_k_norm_g, attn_w_out,
                     conv_w_in, conv_dw_w, conv_dw_b, conv_ln_g, conv_ln_b, conv_w_out)
    y_sample = trunk(x_sample, pre_norm_g, post_norm_g, attn_w_in, attn_q_norm_g, attn_k_norm_g, attn_w_out,
                     conv_w_in, conv_dw_w, conv_dw_b, conv_ln_g, conv_ln_b, conv_w_out)
    return (y_prompt, y_sample)
```

```python
import functools
import math

import jax
import jax.numpy as jnp
from jax import lax
from jax.experimental import pallas as pl
from jax.experimental.pallas import tpu as pltpu

D_MODEL = 1024
HEAD_DIM = 128
N_HEADS = 8
N_KV_HEADS = 2
GQA_GROUP = N_HEADS // N_KV_HEADS
ATTN_WIDTH = N_HEADS * HEAD_DIM
KV_WIDTH = N_KV_HEADS * HEAD_DIM
ATTN_IN_WIDTH = 2 * ATTN_WIDTH + 2 * KV_WIDTH
ROPE_AXIS_DIM = HEAD_DIM // 2
ROPE_THETA = 10000.0
CONV_KERNEL = 31
CONV_PAD = CONV_KERNEL // 2
GRID_W = 64
EPS = 1e-6

SUBLANES = 8
LANES = 128

ATTN_IN_ROWS = 512
KEY_CHUNK = ATTN_IN_ROWS
Q_ROWS = 256
OUT_ROWS = 512
CONV_ROWS = 256
CONV_HALO = 16
CONV_ACC_ROWS = 32

VMEM_LIMIT_BYTES = 56 * 1024 * 1024

F32 = jnp.float32
BF16 = jnp.bfloat16


def _rms_scale(x):
    return lax.rsqrt(jnp.mean(x * x, axis=-1, keepdims=True) + EPS)


def _silu(x):
    return x * jax.nn.sigmoid(x)


def _attn_in_kernel(x_ref, preg_ref, w_ref, gq_ref, gk_ref, cos_ref, sin_ref,
                    q_ref, k_ref, vt_ref, gate_ref):
    x = x_ref[0]
    h = (x * _rms_scale(x) * preg_ref[...]).astype(BF16)
    proj = jnp.dot(h, w_ref[...], preferred_element_type=F32)
    cos = cos_ref[...]
    sin = sin_ref[...]
    lane = lax.broadcasted_iota(jnp.int32, cos.shape, 1)
    first_half = (lane % ROPE_AXIS_DIM) < (ROPE_AXIS_DIM // 2)

    def norm_rope(t, g):
        t = t * _rms_scale(t) * g
        rot = jnp.where(first_half,
                        pltpu.roll(t, HEAD_DIM - ROPE_AXIS_DIM // 2, 1),
                        pltpu.roll(t, ROPE_AXIS_DIM // 2, 1))
        return t * cos + rot * sin

    gq = gq_ref[...]
    gk = gk_ref[...]
    for hd in range(N_HEADS):
        t = proj[:, hd * HEAD_DIM:(hd + 1) * HEAD_DIM]
        q_ref[0, hd] = norm_rope(t, gq).astype(BF16)
    for kv in range(N_KV_HEADS):
        off = ATTN_WIDTH + kv * HEAD_DIM
        k_ref[0, kv] = norm_rope(proj[:, off:off + HEAD_DIM], gk).astype(BF16)
        off = ATTN_WIDTH + KV_WIDTH + kv * HEAD_DIM
        vt_ref[0, kv, 0] = proj[:, off:off + HEAD_DIM].T.astype(BF16)
    z = proj[:, ATTN_WIDTH + 2 * KV_WIDTH:]
    gate_ref[0] = _silu(z).astype(BF16)


def _attn_in(x, pre_g, w_in, gq, gk, cos, sin):
    B, S, _ = x.shape
    T = ATTN_IN_ROWS
    n_t = S // T
    const = lambda b, i: (0, 0)
    return pl.pallas_call(
        _attn_in_kernel,
        grid=(B, n_t),
        in_specs=[
            pl.BlockSpec((1, T, D_MODEL), lambda b, i: (b, i, 0)),
            pl.BlockSpec((1, D_MODEL), const),
            pl.BlockSpec((D_MODEL, ATTN_IN_WIDTH), const),
            pl.BlockSpec((1, HEAD_DIM), const),
            pl.BlockSpec((1, HEAD_DIM), const),
            pl.BlockSpec((T, HEAD_DIM), lambda b, i: (i, 0)),
            pl.BlockSpec((T, HEAD_DIM), lambda b, i: (i, 0)),
        ],
        out_specs=[
            pl.BlockSpec((1, N_HEADS, T, HEAD_DIM), lambda b, i: (b, 0, i, 0)),
            pl.BlockSpec((1, N_KV_HEADS, T, HEAD_DIM), lambda b, i: (b, 0, i, 0)),
            pl.BlockSpec((1, N_KV_HEADS, 1, HEAD_DIM, T), lambda b, i: (b, 0, i, 0, 0)),
            pl.BlockSpec((1, T, D_MODEL), lambda b, i: (b, i, 0)),
        ],
        out_shape=[
            jax.ShapeDtypeStruct((B, N_HEADS, S, HEAD_DIM), BF16),
            jax.ShapeDtypeStruct((B, N_KV_HEADS, S, HEAD_DIM), BF16),
            jax.ShapeDtypeStruct((B, N_KV_HEADS, n_t, HEAD_DIM, T), BF16),
            jax.ShapeDtypeStruct((B, S, D_MODEL), BF16),
        ],
        compiler_params=pltpu.CompilerParams(
            dimension_semantics=("parallel", "parallel"),
            vmem_limit_bytes=VMEM_LIMIT_BYTES),
        name="attn_in",
    )(x, pre_g, w_in, gq, gk, cos, sin)


def _flash_kernel(q_ref, k_ref, vt_ref, o_ref):
    n_chunks = k_ref.shape[2] // KEY_CHUNK
    tq = q_ref.shape[2]
    for hd in range(GQA_GROUP):
        qh = q_ref[0, hd]

        def chunk(c, carry, qh=qh):
            m, l, acc = carry
            start = pl.multiple_of(c * KEY_CHUNK, KEY_CHUNK)
            kc = k_ref[0, 0, pl.ds(start, KEY_CHUNK), :]
            s = lax.dot_general(kc, qh, (((1,), (1,)), ((), ())),
                                preferred_element_type=F32)
            m_new = jnp.maximum(m, jnp.max(s, axis=0, keepdims=True))
            alpha = jnp.exp2(m - m_new)
            p = jnp.exp2(s - m_new)
            l = alpha * l + jnp.sum(p, axis=0, keepdims=True)
            pv = jnp.dot(vt_ref[0, 0, c], p.astype(BF16),
                         preferred_element_type=F32)
            return m_new, l, alpha * acc + pv

        m0 = jnp.full((1, tq), -jnp.inf, F32)
        l0 = jnp.zeros((1, tq), F32)
        a0 = jnp.zeros((HEAD_DIM, tq), F32)
        _, l, acc = lax.fori_loop(0, n_chunks, chunk, (m0, l0, a0))
        o_t = acc * (1.0 / l)
        o_ref[0, :, hd * HEAD_DIM:(hd + 1) * HEAD_DIM] = o_t.T.astype(BF16)


def _flash(q, k, vt):
    B, _, S, _ = q.shape
    tq = Q_ROWS
    n_c = S // KEY_CHUNK
    return pl.pallas_call(
        _flash_kernel,
        grid=(B, N_KV_HEADS, S // tq),
        in_specs=[
            pl.BlockSpec((1, GQA_GROUP, tq, HEAD_DIM), lambda b, g, i: (b, g, i, 0)),
            pl.BlockSpec((1, 1, S, HEAD_DIM), lambda b, g, i: (b, g, 0, 0)),
            pl.BlockSpec((1, 1, n_c, HEAD_DIM, KEY_CHUNK), lambda b, g, i: (b, g, 0, 0, 0)),
        ],
        out_specs=pl.BlockSpec((1, tq, GQA_GROUP * HEAD_DIM), lambda b, g, i: (b, i, g)),
        out_shape=jax.ShapeDtypeStruct((B, S, ATTN_WIDTH), BF16),
        compiler_params=pltpu.CompilerParams(
            dimension_semantics=("parallel", "parallel", "arbitrary"),
            vmem_limit_bytes=VMEM_LIMIT_BYTES),
        name="flash_attn",
    )(q, k, vt)


def _attn_out_kernel(o_ref, gate_ref, x_ref, w_ref, postg_ref, y_ref):
    og = o_ref[0] * gate_ref[0]
    m = jnp.dot(og, w_ref[...], preferred_element_type=F32)
    y_ref[0] = x_ref[0] + m * _rms_scale(m) * postg_ref[...]


def _attn_out(o, gate, x, w_out, post_g):
    B, S, _ = x.shape
    T = OUT_ROWS
    const = lambda b, i: (0, 0)
    row = lambda b, i: (b, i, 0)
    return pl.pallas_call(
        _attn_out_kernel,
        grid=(B, S // T),
        in_specs=[
            pl.BlockSpec((1, T, ATTN_WIDTH), row),
            pl.BlockSpec((1, T, ATTN_WIDTH), row),
            pl.BlockSpec((1, T, D_MODEL), row),
            pl.BlockSpec((ATTN_WIDTH, D_MODEL), const),
            pl.BlockSpec((1, D_MODEL), const),
        ],
        out_specs=pl.BlockSpec((1, T, D_MODEL), row),
        out_shape=jax.ShapeDtypeStruct((B, S, D_MODEL), F32),
        compiler_params=pltpu.CompilerParams(
            dimension_semantics=("parallel", "parallel"),
            vmem_limit_bytes=VMEM_LIMIT_BYTES),
        name="attn_out",
    )(o, gate, x, w_out, post_g)


def _conv_layer_kernel(xl_ref, x_ref, xr_ref, preg_ref, win_ref, dww_ref, dwb_ref,
                       lng_ref, lnb_ref, wout_ref, postg_ref, y_ref,
                       xe_scr, ush_scr, conv_scr, *, seq_len):
    T = x_ref.shape[1]
    H = CONV_HALO
    TE = T + 2 * H
    C = D_MODEL
    i = pl.program_id(1)

    xe_scr[0:H, :] = xl_ref[0]
    xe_scr[H:H + T, :] = x_ref[0]
    xe_scr[H + T:TE, :] = xr_ref[0]
    xe = xe_scr[...]
    h = (xe * _rms_scale(xe) * preg_ref[...]).astype(BF16)

    ag = jnp.dot(h, win_ref[:, 0:2 * C], preferred_element_type=F32)
    u = ag[:, 0:C] * jax.nn.sigmoid(ag[:, C:2 * C])
    t_abs = lax.broadcasted_iota(jnp.int32, (TE, 1), 0) + (i * T - H)
    u = jnp.where((t_abs >= 0) & (t_abs < seq_len), u, 0.0)
    ush_scr[0] = u
    for r in range(1, SUBLANES):
        ush_scr[r, 0:TE - SUBLANES, :] = ush_scr[0, r:r + TE - SUBLANES, :]

    R = CONV_ACC_ROWS

    def conv_rows(rc, carry):
        r0 = pl.multiple_of(rc * R, R)
        acc = jnp.zeros((R, C), F32)
        for kk in range(CONV_KERNEL):
            off = kk + (H - CONV_PAD)
            r = off % SUBLANES
            w = dww_ref[kk]
            w = jnp.concatenate([w] * (R // SUBLANES), axis=0)
            acc = acc + ush_scr[r, pl.ds(r0 + (off - r), R), :] * w
        conv_scr[pl.ds(r0, R), :] = acc + dwb_ref[...]
        return carry

    lax.fori_loop(0, T // R, conv_rows, 0)

    c = conv_scr[...]
    mu = jnp.mean(c, axis=-1, keepdims=True)
    cc = c - mu
    var = jnp.mean(cc * cc, axis=-1, keepdims=True)
    ln = cc * lax.rsqrt(var + EPS) * lng_ref[...] + lnb_ref[...]
    z = jnp.dot(h[H:H + T], win_ref[:, 2 * C:3 * C], preferred_element_type=F32)
    y = (_silu(ln) * _silu(z)).astype(BF16)
    m = jnp.dot(y, wout_ref[...], preferred_element_type=F32)
    y_ref[0] = x_ref[0] + m * _rms_scale(m) * postg_ref[...]


def _conv_layer(x, pre_g, w_in, dw_w8, dw_b, ln_g, ln_b, w_out, post_g):
    B, S, _ = x.shape
    T = CONV_ROWS
    H = CONV_HALO
    hb = T // H
    n_hb = S // H
    const2 = lambda b, i: (0, 0)
    const3 = lambda b, i: (0, 0, 0)
    kernel = functools.partial(_conv_layer_kernel, seq_len=S)
    return pl.pallas_call(
        kernel,
        grid=(B, S // T),
        in_specs=[
            pl.BlockSpec((1, H, D_MODEL), lambda b, i: (b, jnp.maximum(i * hb - 1, 0), 0)),
            pl.BlockSpec((1, T, D_MODEL), lambda b, i: (b, i, 0)),
            pl.BlockSpec((1, H, D_MODEL), lambda b, i: (b, jnp.minimum((i + 1) * hb, n_hb - 1), 0)),
            pl.BlockSpec((1, D_MODEL), const2),
            pl.BlockSpec((D_MODEL, 3 * D_MODEL), const2),
            pl.BlockSpec((CONV_KERNEL, SUBLANES, D_MODEL), const3),
            pl.BlockSpec((1, D_MODEL), const2),
            pl.BlockSpec((1, D_MODEL), const2),
            pl.BlockSpec((1, D_MODEL), const2),
            pl.BlockSpec((D_MODEL, D_MODEL), const2),
            pl.BlockSpec((1, D_MODEL), const2),
        ],
        out_specs=pl.BlockSpec((1, T, D_MODEL), lambda b, i: (b, i, 0)),
        out_shape=jax.ShapeDtypeStruct((B, S, D_MODEL), F32),
        scratch_shapes=[
            pltpu.VMEM((T + 2 * H, D_MODEL), F32),
            pltpu.VMEM((SUBLANES, T + 2 * H, D_MODEL), F32),
            pltpu.VMEM((T, D_MODEL), F32),
        ],
        compiler_params=pltpu.CompilerParams(
            dimension_semantics=("parallel", "parallel"),
            vmem_limit_bytes=VMEM_LIMIT_BYTES),
        name="conv_layer",
    )(x, x, x, pre_g, w_in, dw_w8, dw_b, ln_g, ln_b, w_out, post_g)


def _rope_tables(seq_len):
    rows = seq_len // GRID_W
    row = jnp.repeat(jnp.arange(rows, dtype=F32), GRID_W)
    col = jnp.tile(jnp.arange(GRID_W, dtype=F32), rows)
    inv_freq = ROPE_THETA ** (-jnp.arange(0, ROPE_AXIS_DIM, 2, dtype=F32) / ROPE_AXIS_DIM)
    ang_r = row[:, None] * inv_freq[None, :]
    ang_c = col[:, None] * inv_freq[None, :]
    ang = jnp.concatenate([ang_r, ang_r, ang_c, ang_c], axis=-1)
    half = ROPE_AXIS_DIM // 2
    sign = jnp.where((jnp.arange(HEAD_DIM) % ROPE_AXIS_DIM) < half, -1.0, 1.0).astype(F32)
    return jnp.cos(ang), jnp.sin(ang) * sign[None, :]


def _trunk(x, params):
    (pre_norm_g, post_norm_g, attn_w_in, attn_q_norm_g, attn_k_norm_g, attn_w_out,
     conv_w_in, conv_dw_w, conv_dw_b, conv_ln_g, conv_ln_b, conv_w_out) = params
    S = x.shape[1]
    cos, sin = _rope_tables(S)
    q_scale = HEAD_DIM ** -0.5 * math.log2(math.e)
    depth = pre_norm_g.shape[0]
    for i in range(depth):
        j = i // 2
        pre_g = pre_norm_g[i][None, :]
        post_g = post_norm_g[i][None, :]
        if i % 2 == 0:
            q, k, vt, gate = _attn_in(
                x, pre_g, attn_w_in[j].astype(BF16),
                (attn_q_norm_g[j] * q_scale)[None, :], attn_k_norm_g[j][None, :], cos, sin)
            o = _flash(q, k, vt)
            x = _attn_out(o, gate, x, attn_w_out[j].astype(BF16), post_g)
        else:
            dw_w8 = jnp.broadcast_to(conv_dw_w[j][:, None, :], (CONV_KERNEL, SUBLANES, D_MODEL))
            x = _conv_layer(
                x, pre_g, conv_w_in[j].astype(BF16), dw_w8, conv_dw_b[j][None, :],
                conv_ln_g[j][None, :], conv_ln_b[j][None, :], conv_w_out[j].astype(BF16), post_g)
    return x


def kernel(x_prompt, x_sample, pre_norm_g, post_norm_g, attn_w_in, attn_q_norm_g, attn_k_norm_g,
           attn_w_out, conv_w_in, conv_dw_w, conv_dw_b, conv_ln_g, conv_ln_b, conv_w_out):
    params = (pre_norm_g, post_norm_g, attn_w_in, attn_q_norm_g, attn_k_norm_g, attn_w_out,
              conv_w_in, conv_dw_w, conv_dw_b, conv_ln_g, conv_ln_b, conv_w_out)
    return (_trunk(x_prompt, params), _trunk(x_sample, params))
```

```python
import functools
import math

import jax
import jax.numpy as jnp
from jax import lax
from jax.experimental import pallas as pl
from jax.experimental.pallas import tpu as pltpu

D_MODEL = 1024
HEAD_DIM = 128
N_HEADS = 8
N_KV_HEADS = 2
GQA_GROUP = N_HEADS // N_KV_HEADS
ATTN_WIDTH = N_HEADS * HEAD_DIM
KV_WIDTH = N_KV_HEADS * HEAD_DIM
ATTN_IN_WIDTH = 2 * ATTN_WIDTH + 2 * KV_WIDTH
ROPE_AXIS_DIM = HEAD_DIM // 2
ROPE_THETA = 10000.0
CONV_KERNEL = 31
CONV_PAD = CONV_KERNEL // 2
GRID_W = 64
EPS = 1e-6

SUBLANES = 8
LANES = 128

ATTN_IN_ROWS = 512
KEY_CHUNK = ATTN_IN_ROWS
Q_ROWS = 512
SUB_ROWS = 256
CHUNK_UNROLL = 8
OUT_ROWS = 512
CONV_ROWS = 256
CONV_HALO = 16
CONV_ACC_ROWS = 32

VMEM_LIMIT_BYTES = 56 * 1024 * 1024

F32 = jnp.float32
BF16 = jnp.bfloat16


def _rms_scale(x):
    return lax.rsqrt(jnp.mean(x * x, axis=-1, keepdims=True) + EPS)


def _silu(x):
    return x * jax.nn.sigmoid(x)


def _attn_in_kernel(x_ref, preg_ref, w_ref, gq_ref, gk_ref, cos_ref, sin_ref,
                    q_ref, k_ref, vt_ref, gate_ref):
    R = SUB_ROWS
    lane = lax.broadcasted_iota(jnp.int32, (R, HEAD_DIM), 1)
    first_half = (lane % ROPE_AXIS_DIM) < (ROPE_AXIS_DIM // 2)
    gq = gq_ref[...]
    gk = gk_ref[...]
    for sub in range(x_ref.shape[1] // R):
        rows = slice(sub * R, (sub + 1) * R)
        x = x_ref[0, rows, :]
        h = (x * _rms_scale(x) * preg_ref[...]).astype(BF16)
        proj = jnp.dot(h, w_ref[...], preferred_element_type=F32)
        cos = cos_ref[rows, :]
        sin = sin_ref[rows, :]

        def norm_rope(t, g):
            t = t * _rms_scale(t) * g
            rot = jnp.where(first_half,
                            pltpu.roll(t, HEAD_DIM - ROPE_AXIS_DIM // 2, 1),
                            pltpu.roll(t, ROPE_AXIS_DIM // 2, 1))
            return t * cos + rot * sin

        for hd in range(N_HEADS):
            t = proj[:, hd * HEAD_DIM:(hd + 1) * HEAD_DIM]
            q_ref[0, hd, rows, :] = norm_rope(t, gq).astype(BF16)
        for kv in range(N_KV_HEADS):
            off = ATTN_WIDTH + kv * HEAD_DIM
            k_ref[0, kv, rows, :] = norm_rope(proj[:, off:off + HEAD_DIM], gk).astype(BF16)
            off = ATTN_WIDTH + KV_WIDTH + kv * HEAD_DIM
            vt_ref[0, kv, 0, :, rows] = proj[:, off:off + HEAD_DIM].T.astype(BF16)
        z = proj[:, ATTN_WIDTH + 2 * KV_WIDTH:]
        gate_ref[0, rows, :] = _silu(z).astype(BF16)


def _attn_in(x, pre_g, w_in, gq, gk, cos, sin):
    B, S, _ = x.shape
    T = ATTN_IN_ROWS
    n_t = S // T
    const = lambda b, i: (0, 0)
    return pl.pallas_call(
        _attn_in_kernel,
        grid=(B, n_t),
        in_specs=[
            pl.BlockSpec((1, T, D_MODEL), lambda b, i: (b, i, 0)),
            pl.BlockSpec((1, D_MODEL), const),
            pl.BlockSpec((D_MODEL, ATTN_IN_WIDTH), const),
            pl.BlockSpec((1, HEAD_DIM), const),
            pl.BlockSpec((1, HEAD_DIM), const),
            pl.BlockSpec((T, HEAD_DIM), lambda b, i: (i, 0)),
            pl.BlockSpec((T, HEAD_DIM), lambda b, i: (i, 0)),
        ],
        out_specs=[
            pl.BlockSpec((1, N_HEADS, T, HEAD_DIM), lambda b, i: (b, 0, i, 0)),
            pl.BlockSpec((1, N_KV_HEADS, T, HEAD_DIM), lambda b, i: (b, 0, i, 0)),
            pl.BlockSpec((1, N_KV_HEADS, 1, HEAD_DIM, T), lambda b, i: (b, 0, i, 0, 0)),
            pl.BlockSpec((1, T, D_MODEL), lambda b, i: (b, i, 0)),
        ],
        out_shape=[
            jax.ShapeDtypeStruct((B, N_HEADS, S, HEAD_DIM), BF16),
            jax.ShapeDtypeStruct((B, N_KV_HEADS, S, HEAD_DIM), BF16),
            jax.ShapeDtypeStruct((B, N_KV_HEADS, n_t, HEAD_DIM, T), BF16),
            jax.ShapeDtypeStruct((B, S, D_MODEL), BF16),
        ],
        compiler_params=pltpu.CompilerParams(
            dimension_semantics=("parallel", "parallel"),
            vmem_limit_bytes=VMEM_LIMIT_BYTES),
        name="attn_in",
    )(x, pre_g, w_in, gq, gk, cos, sin)


def _flash_kernel(q_ref, k_ref, vt_ref, o_ref, s_scr, acc_scr):
    seq = k_ref.shape[2]
    n_chunks = seq // KEY_CHUNK
    tq = Q_ROWS
    n_q = seq // tq
    nt_dims = (((1,), (1,)), ((), ()))

    def stage(rows_a, head_a, slot_a, unit_b):
        q_a = q_ref[0, head_a, pl.ds(rows_a, tq), :]
        if unit_b is not None:
            rows_b, head_b, slot_b, m_b = unit_b
            acc_scr[...] = jnp.zeros_like(acc_scr)

        def chunk(c, carry):
            m_a, l_b = carry
            start = pl.multiple_of(c * KEY_CHUNK, KEY_CHUNK)
            s = lax.dot_general(k_ref[0, 0, pl.ds(start, KEY_CHUNK), :], q_a, nt_dims,
                                preferred_element_type=F32)
            s_scr[slot_a, pl.ds(start, KEY_CHUNK), :] = s
            m_a = jnp.maximum(m_a, jnp.max(s, axis=0, keepdims=True))
            if unit_b is not None:
                p = jnp.exp2(s_scr[slot_b, pl.ds(start, KEY_CHUNK), :] - m_b)
                l_b = l_b + jnp.sum(p, axis=0, keepdims=True)
                acc_scr[...] += jnp.dot(vt_ref[0, 0, c], p.astype(BF16),
                                        preferred_element_type=F32)
            return m_a, l_b

        m_a, l_b = lax.fori_loop(
            0, n_chunks, chunk,
            (jnp.full((1, tq), -jnp.inf, F32), jnp.zeros((1, tq), F32)),
            unroll=CHUNK_UNROLL)
        if unit_b is not None:
            o_t = acc_scr[...] * (1.0 / l_b)
            o_ref[0, pl.ds(rows_b, tq), head_b * HEAD_DIM:(head_b + 1) * HEAD_DIM] = (
                o_t.T.astype(BF16))
        return m_a

    def q_block(qi, m_first):
        rows = pl.multiple_of(qi * tq, tq)
        m = m_first
        for hd in range(1, GQA_GROUP):
            m = stage(rows, hd, hd % 2, (rows, hd - 1, (hd - 1) % 2, m))
        rows_next = pl.multiple_of(jnp.minimum(qi + 1, n_q - 1) * tq, tq)
        last = GQA_GROUP - 1
        return stage(rows_next, 0, 0, (rows, last, last % 2, m))

    m0 = stage(0, 0, 0, None)
    lax.fori_loop(0, n_q, q_block, m0)


def _flash(q, k, vt):
    B, _, S, _ = q.shape
    n_c = S // KEY_CHUNK
    gw = GQA_GROUP * HEAD_DIM
    return pl.pallas_call(
        _flash_kernel,
        grid=(B, N_KV_HEADS),
        in_specs=[
            pl.BlockSpec((1, GQA_GROUP, S, HEAD_DIM), lambda b, g: (b, g, 0, 0)),
            pl.BlockSpec((1, 1, S, HEAD_DIM), lambda b, g: (b, g, 0, 0)),
            pl.BlockSpec((1, 1, n_c, HEAD_DIM, KEY_CHUNK), lambda b, g: (b, g, 0, 0, 0)),
        ],
        out_specs=pl.BlockSpec((1, S, gw), lambda b, g: (b, 0, g)),
        out_shape=jax.ShapeDtypeStruct((B, S, ATTN_WIDTH), BF16),
        scratch_shapes=[
            pltpu.VMEM((2, S, Q_ROWS), F32),
            pltpu.VMEM((HEAD_DIM, Q_ROWS), F32),
        ],
        compiler_params=pltpu.CompilerParams(
            dimension_semantics=("parallel", "parallel"),
            vmem_limit_bytes=VMEM_LIMIT_BYTES),
        name="flash_attn",
    )(q, k, vt)


def _attn_out_kernel(o_ref, gate_ref, x_ref, w_ref, postg_ref, y_ref):
    R = SUB_ROWS
    for sub in range(x_ref.shape[1] // R):
        rows = slice(sub * R, (sub + 1) * R)
        og = o_ref[0, rows, :] * gate_ref[0, rows, :]
        m = jnp.dot(og, w_ref[...], preferred_element_type=F32)
        y_ref[0, rows, :] = x_ref[0, rows, :] + m * _rms_scale(m) * postg_ref[...]


def _attn_out(o, gate, x, w_out, post_g):
    B, S, _ = x.shape
    T = OUT_ROWS
    const = lambda b, i: (0, 0)
    row = lambda b, i: (b, i, 0)
    return pl.pallas_call(
        _attn_out_kernel,
        grid=(B, S // T),
        in_specs=[
            pl.BlockSpec((1, T, ATTN_WIDTH), row),
            pl.BlockSpec((1, T, ATTN_WIDTH), row),
            pl.BlockSpec((1, T, D_MODEL), row),
            pl.BlockSpec((ATTN_WIDTH, D_MODEL), const),
            pl.BlockSpec((1, D_MODEL), const),
        ],
        out_specs=pl.BlockSpec((1, T, D_MODEL), row),
        out_shape=jax.ShapeDtypeStruct((B, S, D_MODEL), F32),
        compiler_params=pltpu.CompilerParams(
            dimension_semantics=("parallel", "parallel"),
            vmem_limit_bytes=VMEM_LIMIT_BYTES),
        name="attn_out",
    )(o, gate, x, w_out, post_g)


def _conv_layer_kernel(xl_ref, x_ref, xr_ref, preg_ref, win_ref, dww_ref, dwb_ref,
                       lng_ref, lnb_ref, wout_ref, postg_ref, y_ref,
                       xe_scr, ush_scr, conv_scr, *, seq_len):
    T = x_ref.shape[1]
    H = CONV_HALO
    TE = T + 2 * H
    C = D_MODEL
    i = pl.program_id(1)

    xe_scr[0:H, :] = xl_ref[0]
    xe_scr[H:H + T, :] = x_ref[0]
    xe_scr[H + T:TE, :] = xr_ref[0]
    xe = xe_scr[...]
    h = (xe * _rms_scale(xe) * preg_ref[...]).astype(BF16)

    ag = jnp.dot(h, win_ref[:, 0:2 * C], preferred_element_type=F32)
    u = ag[:, 0:C] * jax.nn.sigmoid(ag[:, C:2 * C])
    t_abs = lax.broadcasted_iota(jnp.int32, (TE, 1), 0) + (i * T - H)
    u = jnp.where((t_abs >= 0) & (t_abs < seq_len), u, 0.0)
    ush_scr[0] = u
    for r in range(1, SUBLANES):
        ush_scr[r, 0:TE - SUBLANES, :] = ush_scr[0, r:r + TE - SUBLANES, :]

    R = CONV_ACC_ROWS

    def conv_rows(rc, carry):
        r0 = pl.multiple_of(rc * R, R)
        groups = R // SUBLANES
        acc = [jnp.zeros((SUBLANES, C), F32)] * groups
        for kk in range(CONV_KERNEL):
            off = kk + (H - CONV_PAD)
            r = off % SUBLANES
            w = dww_ref[kk]
            for gi in range(groups):
                rows = pl.ds(r0 + (off - r) + gi * SUBLANES, SUBLANES)
                acc[gi] = acc[gi] + ush_scr[r, rows, :] * w
        for gi in range(groups):
            conv_scr[pl.ds(r0 + gi * SUBLANES, SUBLANES), :] = acc[gi] + dwb_ref[...]
        return carry

    lax.fori_loop(0, T // R, conv_rows, 0)

    c = conv_scr[...]
    mu = jnp.mean(c, axis=-1, keepdims=True)
    cc = c - mu
    var = jnp.mean(cc * cc, axis=-1, keepdims=True)
    ln = cc * lax.rsqrt(var + EPS) * lng_ref[...] + lnb_ref[...]
    z = jnp.dot(h[H:H + T], win_ref[:, 2 * C:3 * C], preferred_element_type=F32)
    y = (_silu(ln) * _silu(z)).astype(BF16)
    m = jnp.dot(y, wout_ref[...], preferred_element_type=F32)
    y_ref[0] = x_ref[0] + m * _rms_scale(m) * postg_ref[...]


def _conv_layer(x, pre_g, w_in, dw_w8, dw_b, ln_g, ln_b, w_out, post_g):
    B, S, _ = x.shape
    T = CONV_ROWS
    H = CONV_HALO
    hb = T // H
    n_hb = S // H
    const2 = lambda b, i: (0, 0)
    const3 = lambda b, i: (0, 0, 0)
    kernel = functools.partial(_conv_layer_kernel, seq_len=S)
    return pl.pallas_call(
        kernel,
        grid=(B, S // T),
        in_specs=[
            pl.BlockSpec((1, H, D_MODEL), lambda b, i: (b, jnp.maximum(i * hb - 1, 0), 0)),
            pl.BlockSpec((1, T, D_MODEL), lambda b, i: (b, i, 0)),
            pl.BlockSpec((1, H, D_MODEL), lambda b, i: (b, jnp.minimum((i + 1) * hb, n_hb - 1), 0)),
            pl.BlockSpec((1, D_MODEL), const2),
            pl.BlockSpec((D_MODEL, 3 * D_MODEL), const2),
            pl.BlockSpec((CONV_KERNEL, SUBLANES, D_MODEL), const3),
            pl.BlockSpec((1, D_MODEL), const2),
            pl.BlockSpec((1, D_MODEL), const2),
            pl.BlockSpec((1, D_MODEL), const2),
            pl.BlockSpec((D_MODEL, D_MODEL), const2),
            pl.BlockSpec((1, D_MODEL), const2),
        ],
        out_specs=pl.BlockSpec((1, T, D_MODEL), lambda b, i: (b, i, 0)),
        out_shape=jax.ShapeDtypeStruct((B, S, D_MODEL), F32),
        scratch_shapes=[
            pltpu.VMEM((T + 2 * H, D_MODEL), F32),
            pltpu.VMEM((SUBLANES, T + 2 * H, D_MODEL), F32),
            pltpu.VMEM((T, D_MODEL), F32),
        ],
        compiler_params=pltpu.CompilerParams(
            dimension_semantics=("parallel", "parallel"),
            vmem_limit_bytes=VMEM_LIMIT_BYTES),
        name="conv_layer",
    )(x, x, x, pre_g, w_in, dw_w8, dw_b, ln_g, ln_b, w_out, post_g)


def _rope_tables(seq_len):
    rows = seq_len // GRID_W
    row = jnp.repeat(jnp.arange(rows, dtype=F32), GRID_W)
    col = jnp.tile(jnp.arange(GRID_W, dtype=F32), rows)
    inv_freq = ROPE_THETA ** (-jnp.arange(0, ROPE_AXIS_DIM, 2, dtype=F32) / ROPE_AXIS_DIM)
    ang_r = row[:, None] * inv_freq[None, :]
    ang_c = col[:, None] * inv_freq[None, :]
    ang = jnp.concatenate([ang_r, ang_r, ang_c, ang_c], axis=-1)
    half = ROPE_AXIS_DIM // 2
    sign = jnp.where((jnp.arange(HEAD_DIM) % ROPE_AXIS_DIM) < half, -1.0, 1.0).astype(F32)
    return jnp.cos(ang), jnp.sin(ang) * sign[None, :]


def _trunk(x, params):
    (pre_norm_g, post_norm_g, attn_w_in, attn_q_norm_g, attn_k_norm_g, attn_w_out,
     conv_w_in, conv_dw_w, conv_dw_b, conv_ln_g, conv_ln_b, conv_w_out) = params
    S = x.shape[1]
    cos, sin = _rope_tables(S)
    q_scale = HEAD_DIM ** -0.5 * math.log2(math.e)
    depth = pre_norm_g.shape[0]
    for i in range(depth):
        j = i // 2
        pre_g = pre_norm_g[i][None, :]
        post_g = post_norm_g[i][None, :]
        if i % 2 == 0:
            q, k, vt, gate = _attn_in(
                x, pre_g, attn_w_in[j].astype(BF16),
                (attn_q_norm_g[j] * q_scale)[None, :], attn_k_norm_g[j][None, :], cos, sin)
            o = _flash(q, k, vt)
            x = _attn_out(o, gate, x, attn_w_out[j].astype(BF16), post_g)
        else:
            dw_w8 = jnp.broadcast_to(conv_dw_w[j][:, None, :], (CONV_KERNEL, SUBLANES, D_MODEL))
            x = _conv_layer(
                x, pre_g, conv_w_in[j].astype(BF16), dw_w8, conv_dw_b[j][None, :],
                conv_ln_g[j][None, :], conv_ln_b[j][None, :], conv_w_out[j].astype(BF16), post_g)
    return x


def kernel(x_prompt, x_sample, pre_norm_g, post_norm_g, attn_w_in, attn_q_norm_g, attn_k_norm_g,
           attn_w_out, conv_w_in, conv_dw_w, conv_dw_b, conv_ln_g, conv_ln_b, conv_w_out):
    params = (pre_norm_g, post_norm_g, attn_w_in, attn_q_norm_g, attn_k_norm_g, attn_w_out,
              conv_w_in, conv_dw_w, conv_dw_b, conv_ln_g, conv_ln_b, conv_w_out)
    return (_trunk(x_prompt, params), _trunk(x_sample, params))
```

```python
import functools
import math

import jax
import jax.numpy as jnp
from jax import lax
from jax.experimental import pallas as pl
from jax.experimental.pallas import tpu as pltpu

D_MODEL = 1024
HEAD_DIM = 128
N_HEADS = 8
N_KV_HEADS = 2
GQA_GROUP = N_HEADS // N_KV_HEADS
ATTN_WIDTH = N_HEADS * HEAD_DIM
KV_WIDTH = N_KV_HEADS * HEAD_DIM
ATTN_IN_WIDTH = 2 * ATTN_WIDTH + 2 * KV_WIDTH
ROPE_AXIS_DIM = HEAD_DIM // 2
ROPE_THETA = 10000.0
CONV_KERNEL = 31
CONV_PAD = CONV_KERNEL // 2
GRID_W = 64
EPS = 1e-6

SUBLANES = 8
LANES = 128

ATTN_IN_ROWS = 512
KEY_CHUNK = ATTN_IN_ROWS
Q_ROWS = 512
SUB_ROWS = 256
CHUNK_UNROLL = 8
OUT_ROWS = 512
CONV_ROWS = 256
CONV_HALO = 16
CONV_ACC_ROWS = 32
CONV_PIECE_COLS = 256

VMEM_LIMIT_BYTES = 56 * 1024 * 1024

F32 = jnp.float32
BF16 = jnp.bfloat16


def _rms_scale(x):
    return lax.rsqrt(jnp.mean(x * x, axis=-1, keepdims=True) + EPS)


def _silu(x):
    return x * jax.nn.sigmoid(x)


def _attn_in_kernel(x_ref, preg_ref, w_ref, gq_ref, gk_ref, cos_ref, sin_ref,
                    q_ref, k_ref, vt_ref, gate_ref):
    R = SUB_ROWS
    lane = lax.broadcasted_iota(jnp.int32, (R, HEAD_DIM), 1)
    first_half = (lane % ROPE_AXIS_DIM) < (ROPE_AXIS_DIM // 2)
    gq = gq_ref[...]
    gk = gk_ref[...]
    for sub in range(x_ref.shape[1] // R):
        rows = slice(sub * R, (sub + 1) * R)
        x = x_ref[0, rows, :]
        h = (x * _rms_scale(x) * preg_ref[...]).astype(BF16)
        proj = jnp.dot(h, w_ref[...], preferred_element_type=F32)
        cos = cos_ref[rows, :]
        sin = sin_ref[rows, :]

        def norm_rope(t, g):
            t = t * _rms_scale(t) * g
            rot = jnp.where(first_half,
                            pltpu.roll(t, HEAD_DIM - ROPE_AXIS_DIM // 2, 1),
                            pltpu.roll(t, ROPE_AXIS_DIM // 2, 1))
            return t * cos + rot * sin

        for hd in range(N_HEADS):
            t = proj[:, hd * HEAD_DIM:(hd + 1) * HEAD_DIM]
            q_ref[0, hd, rows, :] = norm_rope(t, gq).astype(BF16)
        for kv in range(N_KV_HEADS):
            off = ATTN_WIDTH + kv * HEAD_DIM
            k_ref[0, kv, rows, :] = norm_rope(proj[:, off:off + HEAD_DIM], gk).astype(BF16)
            off = ATTN_WIDTH + KV_WIDTH + kv * HEAD_DIM
            vt_ref[0, kv, 0, :, rows] = proj[:, off:off + HEAD_DIM].T.astype(BF16)
        z = proj[:, ATTN_WIDTH + 2 * KV_WIDTH:]
        gate_ref[0, rows, :] = _silu(z).astype(BF16)


def _attn_in(x, pre_g, w_in, gq, gk, cos, sin):
    B, S, _ = x.shape
    T = ATTN_IN_ROWS
    n_t = S // T
    const = lambda b, i: (0, 0)
    return pl.pallas_call(
        _attn_in_kernel,
        grid=(B, n_t),
        in_specs=[
            pl.BlockSpec((1, T, D_MODEL), lambda b, i: (b, i, 0)),
            pl.BlockSpec((1, D_MODEL), const),
            pl.BlockSpec((D_MODEL, ATTN_IN_WIDTH), const),
            pl.BlockSpec((1, HEAD_DIM), const),
            pl.BlockSpec((1, HEAD_DIM), const),
            pl.BlockSpec((T, HEAD_DIM), lambda b, i: (i, 0)),
            pl.BlockSpec((T, HEAD_DIM), lambda b, i: (i, 0)),
        ],
        out_specs=[
            pl.BlockSpec((1, N_HEADS, T, HEAD_DIM), lambda b, i: (b, 0, i, 0)),
            pl.BlockSpec((1, N_KV_HEADS, T, HEAD_DIM), lambda b, i: (b, 0, i, 0)),
            pl.BlockSpec((1, N_KV_HEADS, 1, HEAD_DIM, T), lambda b, i: (b, 0, i, 0, 0)),
            pl.BlockSpec((1, T, D_MODEL), lambda b, i: (b, i, 0)),
        ],
        out_shape=[
            jax.ShapeDtypeStruct((B, N_HEADS, S, HEAD_DIM), BF16),
            jax.ShapeDtypeStruct((B, N_KV_HEADS, S, HEAD_DIM), BF16),
            jax.ShapeDtypeStruct((B, N_KV_HEADS, n_t, HEAD_DIM, T), BF16),
            jax.ShapeDtypeStruct((B, S, D_MODEL), BF16),
        ],
        compiler_params=pltpu.CompilerParams(
            dimension_semantics=("parallel", "parallel"),
            vmem_limit_bytes=VMEM_LIMIT_BYTES),
        name="attn_in",
    )(x, pre_g, w_in, gq, gk, cos, sin)


def _flash_kernel(q_ref, k_ref, vt_ref, o_ref, s_scr, acc_scr):
    seq = k_ref.shape[2]
    n_chunks = seq // KEY_CHUNK
    tq = Q_ROWS
    n_q = seq // tq
    nt_dims = (((1,), (1,)), ((), ()))

    def stage(rows_a, head_a, slot_a, unit_b):
        q_a = q_ref[0, head_a, pl.ds(rows_a, tq), :]
        if unit_b is not None:
            rows_b, head_b, slot_b, m_b = unit_b
            acc_scr[...] = jnp.zeros_like(acc_scr)

        def chunk(c, carry):
            m_a, l_b = carry
            start = pl.multiple_of(c * KEY_CHUNK, KEY_CHUNK)
            s = lax.dot_general(k_ref[0, 0, pl.ds(start, KEY_CHUNK), :], q_a, nt_dims,
                                preferred_element_type=F32)
            s_scr[slot_a, pl.ds(start, KEY_CHUNK), :] = s
            m_a = jnp.maximum(m_a, jnp.max(s, axis=0, keepdims=True))
            if unit_b is not None:
                p = jnp.exp2(s_scr[slot_b, pl.ds(start, KEY_CHUNK), :] - m_b)
                l_b = l_b + jnp.sum(p, axis=0, keepdims=True)
                acc_scr[...] += jnp.dot(vt_ref[0, 0, c], p.astype(BF16),
                                        preferred_element_type=F32)
            return m_a, l_b

        m_a, l_b = lax.fori_loop(
            0, n_chunks, chunk,
            (jnp.full((1, tq), -jnp.inf, F32), jnp.zeros((1, tq), F32)),
            unroll=CHUNK_UNROLL)
        if unit_b is not None:
            o_t = acc_scr[...] * (1.0 / l_b)
            o_ref[0, pl.ds(rows_b, tq), head_b * HEAD_DIM:(head_b + 1) * HEAD_DIM] = (
                o_t.T.astype(BF16))
        return m_a

    def q_block(qi, m_first):
        rows = pl.multiple_of(qi * tq, tq)
        m = m_first
        for hd in range(1, GQA_GROUP):
            m = stage(rows, hd, hd % 2, (rows, hd - 1, (hd - 1) % 2, m))
        rows_next = pl.multiple_of(jnp.minimum(qi + 1, n_q - 1) * tq, tq)
        last = GQA_GROUP - 1
        return stage(rows_next, 0, 0, (rows, last, last % 2, m))

    m0 = stage(0, 0, 0, None)
    lax.fori_loop(0, n_q, q_block, m0)


def _flash(q, k, vt):
    B, _, S, _ = q.shape
    n_c = S // KEY_CHUNK
    gw = GQA_GROUP * HEAD_DIM
    return pl.pallas_call(
        _flash_kernel,
        grid=(B, N_KV_HEADS),
        in_specs=[
            pl.BlockSpec((1, GQA_GROUP, S, HEAD_DIM), lambda b, g: (b, g, 0, 0)),
            pl.BlockSpec((1, 1, S, HEAD_DIM), lambda b, g: (b, g, 0, 0)),
            pl.BlockSpec((1, 1, n_c, HEAD_DIM, KEY_CHUNK), lambda b, g: (b, g, 0, 0, 0)),
        ],
        out_specs=pl.BlockSpec((1, S, gw), lambda b, g: (b, 0, g)),
        out_shape=jax.ShapeDtypeStruct((B, S, ATTN_WIDTH), BF16),
        scratch_shapes=[
            pltpu.VMEM((2, S, Q_ROWS), F32),
            pltpu.VMEM((HEAD_DIM, Q_ROWS), F32),
        ],
        compiler_params=pltpu.CompilerParams(
            dimension_semantics=("parallel", "parallel"),
            vmem_limit_bytes=VMEM_LIMIT_BYTES),
        name="flash_attn",
    )(q, k, vt)


def _attn_out_kernel(o_ref, gate_ref, x_ref, w_ref, postg_ref, y_ref):
    R = SUB_ROWS
    for sub in range(x_ref.shape[1] // R):
        rows = slice(sub * R, (sub + 1) * R)
        og = o_ref[0, rows, :] * gate_ref[0, rows, :]
        m = jnp.dot(og, w_ref[...], preferred_element_type=F32)
        y_ref[0, rows, :] = x_ref[0, rows, :] + m * _rms_scale(m) * postg_ref[...]


def _attn_out(o, gate, x, w_out, post_g):
    B, S, _ = x.shape
    T = OUT_ROWS
    const = lambda b, i: (0, 0)
    row = lambda b, i: (b, i, 0)
    return pl.pallas_call(
        _attn_out_kernel,
        grid=(B, S // T),
        in_specs=[
            pl.BlockSpec((1, T, ATTN_WIDTH), row),
            pl.BlockSpec((1, T, ATTN_WIDTH), row),
            pl.BlockSpec((1, T, D_MODEL), row),
            pl.BlockSpec((ATTN_WIDTH, D_MODEL), const),
            pl.BlockSpec((1, D_MODEL), const),
        ],
        out_specs=pl.BlockSpec((1, T, D_MODEL), row),
        out_shape=jax.ShapeDtypeStruct((B, S, D_MODEL), F32),
        compiler_params=pltpu.CompilerParams(
            dimension_semantics=("parallel", "parallel"),
            vmem_limit_bytes=VMEM_LIMIT_BYTES),
        name="attn_out",
    )(o, gate, x, w_out, post_g)


def _conv_layer_kernel(xl_ref, x_ref, xr_ref, preg_ref, win_ref, dww_ref, dwb_ref,
                       lng_ref, lnb_ref, wout_ref, postg_ref, y_ref,
                       u_scr, gate_scr, conv_scr, *, seq_len):
    T = x_ref.shape[1]
    H = CONV_HALO
    TE = T + 2 * H
    C = D_MODEL
    W = CONV_PIECE_COLS
    tiles_per_piece = W // LANES
    i = pl.program_id(1)

    def input_norm():
        xe = jnp.concatenate([xl_ref[0], x_ref[0], xr_ref[0]], axis=0)
        return (xe * _rms_scale(xe) * preg_ref[...]).astype(BF16)

    def project_piece(h, p):
        a = jnp.dot(h, win_ref[:, p * W:(p + 1) * W], preferred_element_type=F32)
        g = jnp.dot(h, win_ref[:, C + p * W:C + (p + 1) * W], preferred_element_type=F32)
        u = a * jax.nn.sigmoid(g)
        t_abs = lax.broadcasted_iota(jnp.int32, (TE, 1), 0) + (i * T - H)
        u = jnp.where((t_abs >= 0) & (t_abs < seq_len), u, 0.0)
        for j in range(tiles_per_piece):
            u_scr[p * tiles_per_piece + j] = u[:, j * LANES:(j + 1) * LANES]
        z = jnp.dot(h[H:H + T], win_ref[:, 2 * C + p * W:2 * C + (p + 1) * W],
                    preferred_element_type=F32)
        gate_scr[:, p * W:(p + 1) * W] = _silu(z)

    def conv_taps(ci):
        first_row = H - CONV_PAD
        groups = CONV_ACC_ROWS // SUBLANES
        lanes = slice(ci * LANES, (ci + 1) * LANES)
        bias = jnp.broadcast_to(dwb_ref[:, lanes], (SUBLANES, LANES))
        for rb in range(0, T, CONV_ACC_ROWS):
            acc = [bias] * groups
            for kk in range(CONV_KERNEL):
                w = dww_ref[kk, :, lanes]
                for gi in range(groups):
                    r0 = rb + gi * SUBLANES + first_row + kk
                    acc[gi] = acc[gi] + u_scr[ci, r0:r0 + SUBLANES, :] * w
            for gi in range(groups):
                r0 = rb + gi * SUBLANES
                conv_scr[r0:r0 + SUBLANES, lanes] = acc[gi]

    def finish():
        c = conv_scr[...]
        mu = jnp.mean(c, axis=-1, keepdims=True)
        cc = c - mu
        var = jnp.mean(cc * cc, axis=-1, keepdims=True)
        ln = cc * lax.rsqrt(var + EPS) * lng_ref[...] + lnb_ref[...]
        y = (_silu(ln) * gate_scr[...]).astype(BF16)
        m = jnp.dot(y, wout_ref[...], preferred_element_type=F32)
        y_ref[0] = x_ref[0] + m * _rms_scale(m) * postg_ref[...]

    h = input_norm()
    for p in range(C // W):
        project_piece(h, p)
    for ci in range(C // LANES):
        conv_taps(ci)
    finish()


def _conv_layer(x, pre_g, w_in, dw_w8, dw_b, ln_g, ln_b, w_out, post_g):
    B, S, _ = x.shape
    T = CONV_ROWS
    H = CONV_HALO
    hb = T // H
    n_hb = S // H
    const2 = lambda b, i: (0, 0)
    const3 = lambda b, i: (0, 0, 0)
    kernel = functools.partial(_conv_layer_kernel, seq_len=S)
    return pl.pallas_call(
        kernel,
        grid=(B, S // T),
        in_specs=[
            pl.BlockSpec((1, H, D_MODEL), lambda b, i: (b, jnp.maximum(i * hb - 1, 0), 0)),
            pl.BlockSpec((1, T, D_MODEL), lambda b, i: (b, i, 0)),
            pl.BlockSpec((1, H, D_MODEL), lambda b, i: (b, jnp.minimum((i + 1) * hb, n_hb - 1), 0)),
            pl.BlockSpec((1, D_MODEL), const2),
            pl.BlockSpec((D_MODEL, 3 * D_MODEL), const2),
            pl.BlockSpec((CONV_KERNEL, SUBLANES, D_MODEL), const3),
            pl.BlockSpec((1, D_MODEL), const2),
            pl.BlockSpec((1, D_MODEL), const2),
            pl.BlockSpec((1, D_MODEL), const2),
            pl.BlockSpec((D_MODEL, D_MODEL), const2),
            pl.BlockSpec((1, D_MODEL), const2),
        ],
        out_specs=pl.BlockSpec((1, T, D_MODEL), lambda b, i: (b, i, 0)),
        out_shape=jax.ShapeDtypeStruct((B, S, D_MODEL), F32),
        scratch_shapes=[
            pltpu.VMEM((D_MODEL // LANES, T + 2 * H, LANES), F32),
            pltpu.VMEM((T, D_MODEL), F32),
            pltpu.VMEM((T, D_MODEL), F32),
        ],
        compiler_params=pltpu.CompilerParams(
            dimension_semantics=("parallel", "parallel"),
            vmem_limit_bytes=VMEM_LIMIT_BYTES),
        name="conv_layer",
    )(x, x, x, pre_g, w_in, dw_w8, dw_b, ln_g, ln_b, w_out, post_g)


def _rope_tables(seq_len):
    rows = seq_len // GRID_W
    row = jnp.repeat(jnp.arange(rows, dtype=F32), GRID_W)
    col = jnp.tile(jnp.arange(GRID_W, dtype=F32), rows)
    inv_freq = ROPE_THETA ** (-jnp.arange(0, ROPE_AXIS_DIM, 2, dtype=F32) / ROPE_AXIS_DIM)
    ang_r = row[:, None] * inv_freq[None, :]
    ang_c = col[:, None] * inv_freq[None, :]
    ang = jnp.concatenate([ang_r, ang_r, ang_c, ang_c], axis=-1)
    half = ROPE_AXIS_DIM // 2
    sign = jnp.where((jnp.arange(HEAD_DIM) % ROPE_AXIS_DIM) < half, -1.0, 1.0).astype(F32)
    return jnp.cos(ang), jnp.sin(ang) * sign[None, :]


def _trunk(x, params):
    (pre_norm_g, post_norm_g, attn_w_in, attn_q_norm_g, attn_k_norm_g, attn_w_out,
     conv_w_in, conv_dw_w, conv_dw_b, conv_ln_g, conv_ln_b, conv_w_out) = params
    S = x.shape[1]
    cos, sin = _rope_tables(S)
    q_scale = HEAD_DIM ** -0.5 * math.log2(math.e)
    depth = pre_norm_g.shape[0]
    for i in range(depth):
        j = i // 2
        pre_g = pre_norm_g[i][None, :]
        post_g = post_norm_g[i][None, :]
        if i % 2 == 0:
            q, k, vt, gate = _attn_in(
                x, pre_g, attn_w_in[j].astype(BF16),
                (attn_q_norm_g[j] * q_scale)[None, :], attn_k_norm_g[j][None, :], cos, sin)
            o = _flash(q, k, vt)
            x = _attn_out(o, gate, x, attn_w_out[j].astype(BF16), post_g)
        else:
            dw_w8 = jnp.broadcast_to(conv_dw_w[j][:, None, :], (CONV_KERNEL, SUBLANES, D_MODEL))
            x = _conv_layer(
                x, pre_g, conv_w_in[j].astype(BF16), dw_w8, conv_dw_b[j][None, :],
                conv_ln_g[j][None, :], conv_ln_b[j][None, :], conv_w_out[j].astype(BF16), post_g)
    return x


def kernel(x_prompt, x_sample, pre_norm_g, post_norm_g, attn_w_in, attn_q_norm_g, attn_k_norm_g,
           attn_w_out, conv_w_in, conv_dw_w, conv_dw_b, conv_ln_g, conv_ln_b, conv_w_out):
    params = (pre_norm_g, post_norm_g, attn_w_in, attn_q_norm_g, attn_k_norm_g, attn_w_out,
              conv_w_in, conv_dw_w, conv_dw_b, conv_ln_g, conv_ln_b, conv_w_out)
    return (_trunk(x_prompt, params), _trunk(x_sample, params))
```

```python
import functools
import math

import jax
import jax.numpy as jnp
from jax import lax
from jax.experimental import pallas as pl
from jax.experimental.pallas import tpu as pltpu

D_MODEL = 1024
HEAD_DIM = 128
N_HEADS = 8
N_KV_HEADS = 2
GQA_GROUP = N_HEADS // N_KV_HEADS
ATTN_WIDTH = N_HEADS * HEAD_DIM
KV_WIDTH = N_KV_HEADS * HEAD_DIM
ATTN_IN_WIDTH = 2 * ATTN_WIDTH + 2 * KV_WIDTH
ROPE_AXIS_DIM = HEAD_DIM // 2
ROPE_THETA = 10000.0
CONV_KERNEL = 31
CONV_PAD = CONV_KERNEL // 2
GRID_W = 64
EPS = 1e-6

SUBLANES = 8
LANES = 128

ATTN_IN_ROWS = 1024
KEY_CHUNK = 512
VT_ROWS = HEAD_DIM + 16
Q_ROWS = 512
SUB_ROWS = 256
CHUNK_UNROLL = 8
OUT_ROWS = 512
CONV_ROWS = 256
CONV_HALO = 16
CONV_ACC_ROWS = 32
CONV_PIECE_COLS = 256

VMEM_LIMIT_BYTES = 56 * 1024 * 1024

F32 = jnp.float32
BF16 = jnp.bfloat16


def _rms_scale(x):
    return lax.rsqrt(jnp.mean(x * x, axis=-1, keepdims=True) + EPS)


def _silu(x):
    return x * jax.nn.sigmoid(x)


def _attn_in_kernel(x_ref, preg_ref, w_ref, gq_ref, gk_ref, cos_ref, sin_ref,
                    q_ref, k_ref, vt_ref, gate_ref):
    R = SUB_ROWS
    lane = lax.broadcasted_iota(jnp.int32, (R, HEAD_DIM), 1)
    first_half = (lane % ROPE_AXIS_DIM) < (ROPE_AXIS_DIM // 2)
    gq = gq_ref[...]
    gk = gk_ref[...]
    for sub in range(x_ref.shape[1] // R):
        rows = slice(sub * R, (sub + 1) * R)
        x = x_ref[0, rows, :]
        h = (x * _rms_scale(x) * preg_ref[...]).astype(BF16)
        proj = jnp.dot(h, w_ref[...], preferred_element_type=F32)
        cos = cos_ref[rows, :]
        sin = sin_ref[rows, :]

        def norm_rope(t, g):
            t = t * _rms_scale(t) * g
            rot = jnp.where(first_half,
                            pltpu.roll(t, HEAD_DIM - ROPE_AXIS_DIM // 2, 1),
                            pltpu.roll(t, ROPE_AXIS_DIM // 2, 1))
            return t * cos + rot * sin

        for hd in range(N_HEADS):
            t = proj[:, hd * HEAD_DIM:(hd + 1) * HEAD_DIM]
            q_ref[0, hd, rows, :] = norm_rope(t, gq).astype(BF16)
        for kv in range(N_KV_HEADS):
            off = ATTN_WIDTH + kv * HEAD_DIM
            k_ref[0, kv, rows, :] = norm_rope(proj[:, off:off + HEAD_DIM], gk).astype(BF16)
            off = ATTN_WIDTH + KV_WIDTH + kv * HEAD_DIM
            chunk, col = divmod(sub * R, KEY_CHUNK)
            vt_ref[0, kv, chunk, 0:HEAD_DIM, col:col + R] = proj[:, off:off + HEAD_DIM].T.astype(BF16)
            vt_ref[0, kv, chunk, HEAD_DIM:VT_ROWS, col:col + R] = jnp.ones((VT_ROWS - HEAD_DIM, R), BF16)
        z = proj[:, ATTN_WIDTH + 2 * KV_WIDTH:]
        gate_ref[0, rows, :] = _silu(z).astype(BF16)


def _attn_in(x, pre_g, w_in, gq, gk, cos, sin):
    B, S, _ = x.shape
    T = ATTN_IN_ROWS
    n_t = S // T
    const = lambda b, i: (0, 0)
    return pl.pallas_call(
        _attn_in_kernel,
        grid=(B, n_t),
        in_specs=[
            pl.BlockSpec((1, T, D_MODEL), lambda b, i: (b, i, 0)),
            pl.BlockSpec((1, D_MODEL), const),
            pl.BlockSpec((D_MODEL, ATTN_IN_WIDTH), const),
            pl.BlockSpec((1, HEAD_DIM), const),
            pl.BlockSpec((1, HEAD_DIM), const),
            pl.BlockSpec((T, HEAD_DIM), lambda b, i: (i, 0)),
            pl.BlockSpec((T, HEAD_DIM), lambda b, i: (i, 0)),
        ],
        out_specs=[
            pl.BlockSpec((1, N_HEADS, T, HEAD_DIM), lambda b, i: (b, 0, i, 0)),
            pl.BlockSpec((1, N_KV_HEADS, T, HEAD_DIM), lambda b, i: (b, 0, i, 0)),
            pl.BlockSpec((1, N_KV_HEADS, T // KEY_CHUNK, VT_ROWS, KEY_CHUNK),
                         lambda b, i: (b, 0, i, 0, 0)),
            pl.BlockSpec((1, T, D_MODEL), lambda b, i: (b, i, 0)),
        ],
        out_shape=[
            jax.ShapeDtypeStruct((B, N_HEADS, S, HEAD_DIM), BF16),
            jax.ShapeDtypeStruct((B, N_KV_HEADS, S, HEAD_DIM), BF16),
            jax.ShapeDtypeStruct((B, N_KV_HEADS, S // KEY_CHUNK, VT_ROWS, KEY_CHUNK), BF16),
            jax.ShapeDtypeStruct((B, S, D_MODEL), BF16),
        ],
        compiler_params=pltpu.CompilerParams(
            dimension_semantics=("parallel", "parallel"),
            vmem_limit_bytes=VMEM_LIMIT_BYTES),
        name="attn_in",
    )(x, pre_g, w_in, gq, gk, cos, sin)


def _flash_kernel(q_ref, k_ref, vt_ref, o_ref, s_scr, acc_scr):
    seq = k_ref.shape[2]
    n_chunks = seq // KEY_CHUNK
    tq = Q_ROWS
    n_q = seq // tq
    nt_dims = (((1,), (1,)), ((), ()))

    def stage(rows_a, head_a, slot_a, unit_b):
        q_a = q_ref[0, head_a, pl.ds(rows_a, tq), :]
        if unit_b is not None:
            rows_b, head_b, slot_b, m_b = unit_b
            acc_scr[...] = jnp.zeros_like(acc_scr)

        def chunk(c, carry):
            m_a = carry
            start = pl.multiple_of(c * KEY_CHUNK, KEY_CHUNK)
            s = lax.dot_general(k_ref[0, 0, pl.ds(start, KEY_CHUNK), :], q_a, nt_dims,
                                preferred_element_type=F32)
            s_scr[slot_a, pl.ds(start, KEY_CHUNK), :] = s
            m_a = jnp.maximum(m_a, jnp.max(s, axis=0, keepdims=True))
            if unit_b is not None:
                p = jnp.exp2(s_scr[slot_b, pl.ds(start, KEY_CHUNK), :] - m_b)
                acc_scr[...] += jnp.dot(vt_ref[0, 0, c], p.astype(BF16),
                                        preferred_element_type=F32)
            return m_a

        m_a = lax.fori_loop(0, n_chunks, chunk, jnp.full((1, tq), -jnp.inf, F32),
                            unroll=CHUNK_UNROLL)
        if unit_b is not None:
            o_t = acc_scr[0:HEAD_DIM, :] * (1.0 / acc_scr[HEAD_DIM:HEAD_DIM + 1, :])
            o_ref[0, pl.ds(rows_b, tq), head_b * HEAD_DIM:(head_b + 1) * HEAD_DIM] = (
                o_t.T.astype(BF16))
        return m_a

    def q_block(qi, m_first):
        rows = pl.multiple_of(qi * tq, tq)
        m = m_first
        for hd in range(1, GQA_GROUP):
            m = stage(rows, hd, hd % 2, (rows, hd - 1, (hd - 1) % 2, m))
        rows_next = pl.multiple_of(jnp.minimum(qi + 1, n_q - 1) * tq, tq)
        last = GQA_GROUP - 1
        return stage(rows_next, 0, 0, (rows, last, last % 2, m))

    m0 = stage(0, 0, 0, None)
    lax.fori_loop(0, n_q, q_block, m0)


def _flash(q, k, vt):
    B, _, S, _ = q.shape
    n_c = S // KEY_CHUNK
    gw = GQA_GROUP * HEAD_DIM
    return pl.pallas_call(
        _flash_kernel,
        grid=(B, N_KV_HEADS),
        in_specs=[
            pl.BlockSpec((1, GQA_GROUP, S, HEAD_DIM), lambda b, g: (b, g, 0, 0)),
            pl.BlockSpec((1, 1, S, HEAD_DIM), lambda b, g: (b, g, 0, 0)),
            pl.BlockSpec((1, 1, n_c, VT_ROWS, KEY_CHUNK), lambda b, g: (b, g, 0, 0, 0)),
        ],
        out_specs=pl.BlockSpec((1, S, gw), lambda b, g: (b, 0, g)),
        out_shape=jax.ShapeDtypeStruct((B, S, ATTN_WIDTH), BF16),
        scratch_shapes=[
            pltpu.VMEM((2, S, Q_ROWS), F32),
            pltpu.VMEM((VT_ROWS, Q_ROWS), F32),
        ],
        compiler_params=pltpu.CompilerParams(
            dimension_semantics=("parallel", "parallel"),
            vmem_limit_bytes=VMEM_LIMIT_BYTES),
        name="flash_attn",
    )(q, k, vt)


def _attn_out_kernel(o_ref, gate_ref, x_ref, w_ref, postg_ref, y_ref):
    R = SUB_ROWS
    for sub in range(x_ref.shape[1] // R):
        rows = slice(sub * R, (sub + 1) * R)
        og = o_ref[0, rows, :] * gate_ref[0, rows, :]
        m = jnp.dot(og, w_ref[...], preferred_element_type=F32)
        y_ref[0, rows, :] = x_ref[0, rows, :] + m * _rms_scale(m) * postg_ref[...]


def _attn_out(o, gate, x, w_out, post_g):
    B, S, _ = x.shape
    T = OUT_ROWS
    const = lambda b, i: (0, 0)
    row = lambda b, i: (b, i, 0)
    return pl.pallas_call(
        _attn_out_kernel,
        grid=(B, S // T),
        in_specs=[
            pl.BlockSpec((1, T, ATTN_WIDTH), row),
            pl.BlockSpec((1, T, ATTN_WIDTH), row),
            pl.BlockSpec((1, T, D_MODEL), row),
            pl.BlockSpec((ATTN_WIDTH, D_MODEL), const),
            pl.BlockSpec((1, D_MODEL), const),
        ],
        out_specs=pl.BlockSpec((1, T, D_MODEL), row),
        out_shape=jax.ShapeDtypeStruct((B, S, D_MODEL), F32),
        compiler_params=pltpu.CompilerParams(
            dimension_semantics=("parallel", "parallel"),
            vmem_limit_bytes=VMEM_LIMIT_BYTES),
        name="attn_out",
    )(o, gate, x, w_out, post_g)


def _conv_layer_kernel(xl_ref, x_ref, xr_ref, preg_ref, win_ref, dww_ref, dwb_ref,
                       lng_ref, lnb_ref, wout_ref, postg_ref, y_ref,
                       u_scr, gate_scr, conv_scr, *, seq_len):
    T = x_ref.shape[1]
    H = CONV_HALO
    TE = T + 2 * H
    C = D_MODEL
    W = CONV_PIECE_COLS
    tiles_per_piece = W // LANES
    i = pl.program_id(1)

    def input_norm():
        xe = jnp.concatenate([xl_ref[0], x_ref[0], xr_ref[0]], axis=0)
        return (xe * _rms_scale(xe) * preg_ref[...]).astype(BF16)

    def project_piece(h, p):
        a = jnp.dot(h, win_ref[:, p * W:(p + 1) * W], preferred_element_type=F32)
        g = jnp.dot(h, win_ref[:, C + p * W:C + (p + 1) * W], preferred_element_type=F32)
        u = a * jax.nn.sigmoid(g)
        t_abs = lax.broadcasted_iota(jnp.int32, (TE, 1), 0) + (i * T - H)
        u = jnp.where((t_abs >= 0) & (t_abs < seq_len), u, 0.0)
        for j in range(tiles_per_piece):
            u_scr[p * tiles_per_piece + j] = u[:, j * LANES:(j + 1) * LANES]
        z = jnp.dot(h[H:H + T], win_ref[:, 2 * C + p * W:2 * C + (p + 1) * W],
                    preferred_element_type=F32)
        gate_scr[:, p * W:(p + 1) * W] = _silu(z)

    def conv_taps(ci):
        first_row = H - CONV_PAD
        groups = CONV_ACC_ROWS // SUBLANES
        lanes = slice(ci * LANES, (ci + 1) * LANES)
        bias = jnp.broadcast_to(dwb_ref[:, lanes], (SUBLANES, LANES))
        for rb in range(0, T, CONV_ACC_ROWS):
            acc = [bias] * groups
            for kk in range(CONV_KERNEL):
                w = dww_ref[kk, :, lanes]
                for gi in range(groups):
                    r0 = rb + gi * SUBLANES + first_row + kk
                    acc[gi] = acc[gi] + u_scr[ci, r0:r0 + SUBLANES, :] * w
            for gi in range(groups):
                r0 = rb + gi * SUBLANES
                conv_scr[r0:r0 + SUBLANES, lanes] = acc[gi]

    def finish():
        c = conv_scr[...]
        mu = jnp.mean(c, axis=-1, keepdims=True)
        cc = c - mu
        var = jnp.mean(cc * cc, axis=-1, keepdims=True)
        ln = cc * lax.rsqrt(var + EPS) * lng_ref[...] + lnb_ref[...]
        y = (_silu(ln) * gate_scr[...]).astype(BF16)
        m = jnp.dot(y, wout_ref[...], preferred_element_type=F32)
        y_ref[0] = x_ref[0] + m * _rms_scale(m) * postg_ref[...]

    h = input_norm()
    for p in range(C // W):
        project_piece(h, p)
    for ci in range(C // LANES):
        conv_taps(ci)
    finish()


def _conv_layer(x, pre_g, w_in, dw_w8, dw_b, ln_g, ln_b, w_out, post_g):
    B, S, _ = x.shape
    T = CONV_ROWS
    H = CONV_HALO
    hb = T // H
    n_hb = S // H
    const2 = lambda b, i: (0, 0)
    const3 = lambda b, i: (0, 0, 0)
    kernel = functools.partial(_conv_layer_kernel, seq_len=S)
    return pl.pallas_call(
        kernel,
        grid=(B, S // T),
        in_specs=[
            pl.BlockSpec((1, H, D_MODEL), lambda b, i: (b, jnp.maximum(i * hb - 1, 0), 0)),
            pl.BlockSpec((1, T, D_MODEL), lambda b, i: (b, i, 0)),
            pl.BlockSpec((1, H, D_MODEL), lambda b, i: (b, jnp.minimum((i + 1) * hb, n_hb - 1), 0)),
            pl.BlockSpec((1, D_MODEL), const2),
            pl.BlockSpec((D_MODEL, 3 * D_MODEL), const2),
            pl.BlockSpec((CONV_KERNEL, SUBLANES, D_MODEL), const3),
            pl.BlockSpec((1, D_MODEL), const2),
            pl.BlockSpec((1, D_MODEL), const2),
            pl.BlockSpec((1, D_MODEL), const2),
            pl.BlockSpec((D_MODEL, D_MODEL), const2),
            pl.BlockSpec((1, D_MODEL), const2),
        ],
        out_specs=pl.BlockSpec((1, T, D_MODEL), lambda b, i: (b, i, 0)),
        out_shape=jax.ShapeDtypeStruct((B, S, D_MODEL), F32),
        scratch_shapes=[
            pltpu.VMEM((D_MODEL // LANES, T + 2 * H, LANES), F32),
            pltpu.VMEM((T, D_MODEL), F32),
            pltpu.VMEM((T, D_MODEL), F32),
        ],
        compiler_params=pltpu.CompilerParams(
            dimension_semantics=("parallel", "parallel"),
            vmem_limit_bytes=VMEM_LIMIT_BYTES),
        name="conv_layer",
    )(x, x, x, pre_g, w_in, dw_w8, dw_b, ln_g, ln_b, w_out, post_g)


def _rope_tables(seq_len):
    rows = seq_len // GRID_W
    row = jnp.repeat(jnp.arange(rows, dtype=F32), GRID_W)
    col = jnp.tile(jnp.arange(GRID_W, dtype=F32), rows)
    inv_freq = ROPE_THETA ** (-jnp.arange(0, ROPE_AXIS_DIM, 2, dtype=F32) / ROPE_AXIS_DIM)
    ang_r = row[:, None] * inv_freq[None, :]
    ang_c = col[:, None] * inv_freq[None, :]
    ang = jnp.concatenate([ang_r, ang_r, ang_c, ang_c], axis=-1)
    half = ROPE_AXIS_DIM // 2
    sign = jnp.where((jnp.arange(HEAD_DIM) % ROPE_AXIS_DIM) < half, -1.0, 1.0).astype(F32)
    return jnp.cos(ang), jnp.sin(ang) * sign[None, :]


def _trunk(x, params):
    (pre_norm_g, post_norm_g, attn_w_in, attn_q_norm_g, attn_k_norm_g, attn_w_out,
     conv_w_in, conv_dw_w, conv_dw_b, conv_ln_g, conv_ln_b, conv_w_out) = params
    S = x.shape[1]
    cos, sin = _rope_tables(S)
    q_scale = HEAD_DIM ** -0.5 * math.log2(math.e)
    depth = pre_norm_g.shape[0]
    for i in range(depth):
        j = i // 2
        pre_g = pre_norm_g[i][None, :]
        post_g = post_norm_g[i][None, :]
        if i % 2 == 0:
            q, k, vt, gate = _attn_in(
                x, pre_g, attn_w_in[j].astype(BF16),
                (attn_q_norm_g[j] * q_scale)[None, :], attn_k_norm_g[j][None, :], cos, sin)
            o = _flash(q, k, vt)
            x = _attn_out(o, gate, x, attn_w_out[j].astype(BF16), post_g)
        else:
            dw_w8 = jnp.broadcast_to(conv_dw_w[j][:, None, :], (CONV_KERNEL, SUBLANES, D_MODEL))
            x = _conv_layer(
                x, pre_g, conv_w_in[j].astype(BF16), dw_w8, conv_dw_b[j][None, :],
                conv_ln_g[j][None, :], conv_ln_b[j][None, :], conv_w_out[j].astype(BF16), post_g)
    return x


def kernel(x_prompt, x_sample, pre_norm_g, post_norm_g, attn_w_in, attn_q_norm_g, attn_k_norm_g,
           attn_w_out, conv_w_in, conv_dw_w, conv_dw_b, conv_ln_g, conv_ln_b, conv_w_out):
    params = (pre_norm_g, post_norm_g, attn_w_in, attn_q_norm_g, attn_k_norm_g, attn_w_out,
              conv_w_in, conv_dw_w, conv_dw_b, conv_ln_g, conv_ln_b, conv_w_out)
    return (_trunk(x_prompt, params), _trunk(x_sample, params))
```

```python
import functools
import math

import jax
import jax.numpy as jnp
from jax import lax
from jax.experimental import pallas as pl
from jax.experimental.pallas import tpu as pltpu

D_MODEL = 1024
HEAD_DIM = 128
N_HEADS = 8
N_KV_HEADS = 2
GQA_GROUP = N_HEADS // N_KV_HEADS
ATTN_WIDTH = N_HEADS * HEAD_DIM
KV_WIDTH = N_KV_HEADS * HEAD_DIM
ATTN_IN_WIDTH = 2 * ATTN_WIDTH + 2 * KV_WIDTH
ROPE_AXIS_DIM = HEAD_DIM // 2
ROPE_THETA = 10000.0
CONV_KERNEL = 31
CONV_PAD = CONV_KERNEL // 2
GRID_W = 64
EPS = 1e-6

SUBLANES = 8
LANES = 128

ATTN_IN_ROWS = 1024
KEY_CHUNK = 512
VT_ROWS = HEAD_DIM + 16
Q_ROWS = 512
SUB_ROWS = 256
CHUNK_UNROLL = 8
OUT_ROWS = 1024
CONV_ROWS = 512
CONV_HALO = 16
CONV_ACC_ROWS = 32
CONV_PIECE_COLS = 256

VMEM_LIMIT_BYTES = 56 * 1024 * 1024

F32 = jnp.float32
BF16 = jnp.bfloat16


def _rms_scale(x):
    return lax.rsqrt(jnp.mean(x * x, axis=-1, keepdims=True) + EPS)


def _silu(x):
    return x * jax.nn.sigmoid(x)


def _attn_in_kernel(x_ref, preg_ref, w_ref, gq_ref, gk_ref, cos_ref, sin_ref,
                    q_ref, k_ref, vt_ref, gate_ref):
    R = SUB_ROWS
    lane = lax.broadcasted_iota(jnp.int32, (R, HEAD_DIM), 1)
    first_half = (lane % ROPE_AXIS_DIM) < (ROPE_AXIS_DIM // 2)
    gq = gq_ref[...]
    gk = gk_ref[...]
    for sub in range(x_ref.shape[1] // R):
        rows = slice(sub * R, (sub + 1) * R)
        x = x_ref[0, rows, :]
        h = (x * _rms_scale(x) * preg_ref[...]).astype(BF16)
        proj = jnp.dot(h, w_ref[...], preferred_element_type=F32)
        cos = cos_ref[rows, :]
        sin = sin_ref[rows, :]

        def norm_rope(t, g):
            t = t * _rms_scale(t) * g
            rot = jnp.where(first_half,
                            pltpu.roll(t, HEAD_DIM - ROPE_AXIS_DIM // 2, 1),
                            pltpu.roll(t, ROPE_AXIS_DIM // 2, 1))
            return t * cos + rot * sin

        for hd in range(N_HEADS):
            t = proj[:, hd * HEAD_DIM:(hd + 1) * HEAD_DIM]
            q_ref[0, hd, rows, :] = norm_rope(t, gq).astype(BF16)
        for kv in range(N_KV_HEADS):
            off = ATTN_WIDTH + kv * HEAD_DIM
            k_ref[0, kv, rows, :] = norm_rope(proj[:, off:off + HEAD_DIM], gk).astype(BF16)
            off = ATTN_WIDTH + KV_WIDTH + kv * HEAD_DIM
            chunk, col = divmod(sub * R, KEY_CHUNK)
            vt_ref[0, kv, chunk, 0:HEAD_DIM, col:col + R] = proj[:, off:off + HEAD_DIM].T.astype(BF16)
            vt_ref[0, kv, chunk, HEAD_DIM:VT_ROWS, col:col + R] = jnp.ones((VT_ROWS - HEAD_DIM, R), BF16)
        z = proj[:, ATTN_WIDTH + 2 * KV_WIDTH:]
        gate_ref[0, rows, :] = _silu(z).astype(BF16)


def _attn_in(x, pre_g, w_in, gq, gk, cos, sin):
    B, S, _ = x.shape
    T = ATTN_IN_ROWS
    n_t = S // T
    const = lambda b, i: (0, 0)
    return pl.pallas_call(
        _attn_in_kernel,
        grid=(B, n_t),
        in_specs=[
            pl.BlockSpec((1, T, D_MODEL), lambda b, i: (b, i, 0)),
            pl.BlockSpec((1, D_MODEL), const),
            pl.BlockSpec((D_MODEL, ATTN_IN_WIDTH), const),
            pl.BlockSpec((1, HEAD_DIM), const),
            pl.BlockSpec((1, HEAD_DIM), const),
            pl.BlockSpec((T, HEAD_DIM), lambda b, i: (i, 0)),
            pl.BlockSpec((T, HEAD_DIM), lambda b, i: (i, 0)),
        ],
        out_specs=[
            pl.BlockSpec((1, N_HEADS, T, HEAD_DIM), lambda b, i: (b, 0, i, 0)),
            pl.BlockSpec((1, N_KV_HEADS, T, HEAD_DIM), lambda b, i: (b, 0, i, 0)),
            pl.BlockSpec((1, N_KV_HEADS, T // KEY_CHUNK, VT_ROWS, KEY_CHUNK),
                         lambda b, i: (b, 0, i, 0, 0)),
            pl.BlockSpec((1, T, D_MODEL), lambda b, i: (b, i, 0)),
        ],
        out_shape=[
            jax.ShapeDtypeStruct((B, N_HEADS, S, HEAD_DIM), BF16),
            jax.ShapeDtypeStruct((B, N_KV_HEADS, S, HEAD_DIM), BF16),
            jax.ShapeDtypeStruct((B, N_KV_HEADS, S // KEY_CHUNK, VT_ROWS, KEY_CHUNK), BF16),
            jax.ShapeDtypeStruct((B, S, D_MODEL), BF16),
        ],
        compiler_params=pltpu.CompilerParams(
            dimension_semantics=("parallel", "parallel"),
            vmem_limit_bytes=VMEM_LIMIT_BYTES),
        name="attn_in",
    )(x, pre_g, w_in, gq, gk, cos, sin)


def _flash_kernel(q_ref, k_ref, vt_ref, o_ref, s_scr, acc_scr):
    seq = k_ref.shape[2]
    n_chunks = seq // KEY_CHUNK
    tq = Q_ROWS
    n_q = seq // tq
    nt_dims = (((1,), (1,)), ((), ()))

    def stage(rows_a, head_a, slot_a, unit_b):
        q_a = q_ref[0, head_a, pl.ds(rows_a, tq), :]
        if unit_b is not None:
            rows_b, head_b, slot_b, m_b = unit_b
            acc_scr[...] = jnp.zeros_like(acc_scr)

        def chunk(c, carry):
            m_a = carry
            start = pl.multiple_of(c * KEY_CHUNK, KEY_CHUNK)
            s = lax.dot_general(k_ref[0, 0, pl.ds(start, KEY_CHUNK), :], q_a, nt_dims,
                                preferred_element_type=F32)
            s_scr[slot_a, pl.ds(start, KEY_CHUNK), :] = s
            m_a = jnp.maximum(m_a, jnp.max(s, axis=0, keepdims=True))
            if unit_b is not None:
                p = jnp.exp2(s_scr[slot_b, pl.ds(start, KEY_CHUNK), :] - m_b)
                acc_scr[...] += jnp.dot(vt_ref[0, 0, c], p.astype(BF16),
                                        preferred_element_type=F32)
            return m_a

        m_a = lax.fori_loop(0, n_chunks, chunk, jnp.full((1, tq), -jnp.inf, F32),
                            unroll=CHUNK_UNROLL)
        if unit_b is not None:
            o_t = acc_scr[0:HEAD_DIM, :] * (1.0 / acc_scr[HEAD_DIM:HEAD_DIM + 1, :])
            o_ref[0, pl.ds(rows_b, tq), head_b * HEAD_DIM:(head_b + 1) * HEAD_DIM] = (
                o_t.T.astype(BF16))
        return m_a

    def q_block(qi, m_first):
        rows = pl.multiple_of(qi * tq, tq)
        m = m_first
        for hd in range(1, GQA_GROUP):
            m = stage(rows, hd, hd % 2, (rows, hd - 1, (hd - 1) % 2, m))
        rows_next = pl.multiple_of(jnp.minimum(qi + 1, n_q - 1) * tq, tq)
        last = GQA_GROUP - 1
        return stage(rows_next, 0, 0, (rows, last, last % 2, m))

    m0 = stage(0, 0, 0, None)
    lax.fori_loop(0, n_q, q_block, m0)


def _flash(q, k, vt):
    B, _, S, _ = q.shape
    n_c = S // KEY_CHUNK
    gw = GQA_GROUP * HEAD_DIM
    return pl.pallas_call(
        _flash_kernel,
        grid=(B, N_KV_HEADS),
        in_specs=[
            pl.BlockSpec((1, GQA_GROUP, S, HEAD_DIM), lambda b, g: (b, g, 0, 0)),
            pl.BlockSpec((1, 1, S, HEAD_DIM), lambda b, g: (b, g, 0, 0)),
            pl.BlockSpec((1, 1, n_c, VT_ROWS, KEY_CHUNK), lambda b, g: (b, g, 0, 0, 0)),
        ],
        out_specs=pl.BlockSpec((1, S, gw), lambda b, g: (b, 0, g)),
        out_shape=jax.ShapeDtypeStruct((B, S, ATTN_WIDTH), BF16),
        scratch_shapes=[
            pltpu.VMEM((2, S, Q_ROWS), F32),
            pltpu.VMEM((VT_ROWS, Q_ROWS), F32),
        ],
        compiler_params=pltpu.CompilerParams(
            dimension_semantics=("parallel", "parallel"),
            vmem_limit_bytes=VMEM_LIMIT_BYTES),
        name="flash_attn",
    )(q, k, vt)


def _attn_out_kernel(o_ref, gate_ref, x_ref, w_ref, postg_ref, y_ref):
    R = SUB_ROWS
    for sub in range(x_ref.shape[1] // R):
        rows = slice(sub * R, (sub + 1) * R)
        og = o_ref[0, rows, :] * gate_ref[0, rows, :]
        m = jnp.dot(og, w_ref[...], preferred_element_type=F32)
        y_ref[0, rows, :] = x_ref[0, rows, :] + m * _rms_scale(m) * postg_ref[...]


def _attn_out(o, gate, x, w_out, post_g):
    B, S, _ = x.shape
    T = OUT_ROWS
    const = lambda b, i: (0, 0)
    row = lambda b, i: (b, i, 0)
    return pl.pallas_call(
        _attn_out_kernel,
        grid=(B, S // T),
        in_specs=[
            pl.BlockSpec((1, T, ATTN_WIDTH), row),
            pl.BlockSpec((1, T, ATTN_WIDTH), row),
            pl.BlockSpec((1, T, D_MODEL), row),
            pl.BlockSpec((ATTN_WIDTH, D_MODEL), const),
            pl.BlockSpec((1, D_MODEL), const),
        ],
        out_specs=pl.BlockSpec((1, T, D_MODEL), row),
        out_shape=jax.ShapeDtypeStruct((B, S, D_MODEL), F32),
        compiler_params=pltpu.CompilerParams(
            dimension_semantics=("parallel", "parallel"),
            vmem_limit_bytes=VMEM_LIMIT_BYTES),
        name="attn_out",
    )(o, gate, x, w_out, post_g)


def _conv_layer_kernel(xl_ref, x_ref, xr_ref, preg_ref, win_ref, dww_ref, dwb_ref,
                       lng_ref, lnb_ref, wout_ref, postg_ref, y_ref,
                       u_scr, gate_scr, conv_scr, *, seq_len):
    T = x_ref.shape[1]
    H = CONV_HALO
    TE = T + 2 * H
    C = D_MODEL
    W = CONV_PIECE_COLS
    tiles_per_piece = W // LANES
    i = pl.program_id(1)

    def input_norm():
        xe = jnp.concatenate([xl_ref[0], x_ref[0], xr_ref[0]], axis=0)
        return (xe * _rms_scale(xe) * preg_ref[...]).astype(BF16)

    def project_piece(h, p):
        a = jnp.dot(h, win_ref[:, p * W:(p + 1) * W], preferred_element_type=F32)
        g = jnp.dot(h, win_ref[:, C + p * W:C + (p + 1) * W], preferred_element_type=F32)
        u = a * jax.nn.sigmoid(g)
        t_abs = lax.broadcasted_iota(jnp.int32, (TE, 1), 0) + (i * T - H)
        u = jnp.where((t_abs >= 0) & (t_abs < seq_len), u, 0.0)
        for j in range(tiles_per_piece):
            u_scr[p * tiles_per_piece + j] = u[:, j * LANES:(j + 1) * LANES]
        z = jnp.dot(h[H:H + T], win_ref[:, 2 * C + p * W:2 * C + (p + 1) * W],
                    preferred_element_type=F32)
        gate_scr[:, p * W:(p + 1) * W] = _silu(z)

    def conv_taps(ci):
        first_row = H - CONV_PAD
        groups = CONV_ACC_ROWS // SUBLANES
        lanes = slice(ci * LANES, (ci + 1) * LANES)
        bias = jnp.broadcast_to(dwb_ref[:, lanes], (SUBLANES, LANES))
        for rb in range(0, T, CONV_ACC_ROWS):
            acc = [bias] * groups
            for kk in range(CONV_KERNEL):
                w = dww_ref[kk, :, lanes]
                for gi in range(groups):
                    r0 = rb + gi * SUBLANES + first_row + kk
                    acc[gi] = acc[gi] + u_scr[ci, r0:r0 + SUBLANES, :] * w
            for gi in range(groups):
                r0 = rb + gi * SUBLANES
                conv_scr[r0:r0 + SUBLANES, lanes] = acc[gi]

    def finish():
        c = conv_scr[...]
        mu = jnp.mean(c, axis=-1, keepdims=True)
        cc = c - mu
        var = jnp.mean(cc * cc, axis=-1, keepdims=True)
        ln = cc * lax.rsqrt(var + EPS) * lng_ref[...] + lnb_ref[...]
        y = (_silu(ln) * gate_scr[...]).astype(BF16)
        m = jnp.dot(y, wout_ref[...], preferred_element_type=F32)
        y_ref[0] = x_ref[0] + m * _rms_scale(m) * postg_ref[...]

    h = input_norm()
    for p in range(C // W):
        project_piece(h, p)
    for ci in range(C // LANES):
        conv_taps(ci)
    finish()


def _conv_layer(x, pre_g, w_in, dw_w8, dw_b, ln_g, ln_b, w_out, post_g):
    B, S, _ = x.shape
    T = CONV_ROWS
    H = CONV_HALO
    hb = T // H
    n_hb = S // H
    const2 = lambda b, i: (0, 0)
    const3 = lambda b, i: (0, 0, 0)
    kernel = functools.partial(_conv_layer_kernel, seq_len=S)
    return pl.pallas_call(
        kernel,
        grid=(B, S // T),
        in_specs=[
            pl.BlockSpec((1, H, D_MODEL), lambda b, i: (b, jnp.maximum(i * hb - 1, 0), 0)),
            pl.BlockSpec((1, T, D_MODEL), lambda b, i: (b, i, 0)),
            pl.BlockSpec((1, H, D_MODEL), lambda b, i: (b, jnp.minimum((i + 1) * hb, n_hb - 1), 0)),
            pl.BlockSpec((1, D_MODEL), const2),
            pl.BlockSpec((D_MODEL, 3 * D_MODEL), const2),
            pl.BlockSpec((CONV_KERNEL, SUBLANES, D_MODEL), const3),
            pl.BlockSpec((1, D_MODEL), const2),
            pl.BlockSpec((1, D_MODEL), const2),
            pl.BlockSpec((1, D_MODEL), const2),
            pl.BlockSpec((D_MODEL, D_MODEL), const2),
            pl.BlockSpec((1, D_MODEL), const2),
        ],
        out_specs=pl.BlockSpec((1, T, D_MODEL), lambda b, i: (b, i, 0)),
        out_shape=jax.ShapeDtypeStruct((B, S, D_MODEL), F32),
        scratch_shapes=[
            pltpu.VMEM((D_MODEL // LANES, T + 2 * H, LANES), F32),
            pltpu.VMEM((T, D_MODEL), F32),
            pltpu.VMEM((T, D_MODEL), F32),
        ],
        compiler_params=pltpu.CompilerParams(
            dimension_semantics=("parallel", "parallel"),
            vmem_limit_bytes=VMEM_LIMIT_BYTES),
        name="conv_layer",
    )(x, x, x, pre_g, w_in, dw_w8, dw_b, ln_g, ln_b, w_out, post_g)


def _rope_tables(seq_len):
    rows = seq_len // GRID_W
    row = jnp.repeat(jnp.arange(rows, dtype=F32), GRID_W)
    col = jnp.tile(jnp.arange(GRID_W, dtype=F32), rows)
    inv_freq = ROPE_THETA ** (-jnp.arange(0, ROPE_AXIS_DIM, 2, dtype=F32) / ROPE_AXIS_DIM)
    ang_r = row[:, None] * inv_freq[None, :]
    ang_c = col[:, None] * inv_freq[None, :]
    ang = jnp.concatenate([ang_r, ang_r, ang_c, ang_c], axis=-1)
    half = ROPE_AXIS_DIM // 2
    sign = jnp.where((jnp.arange(HEAD_DIM) % ROPE_AXIS_DIM) < half, -1.0, 1.0).astype(F32)
    return jnp.cos(ang), jnp.sin(ang) * sign[None, :]


def _trunk(x, params):
    (pre_norm_g, post_norm_g, attn_w_in, attn_q_norm_g, attn_k_norm_g, attn_w_out,
     conv_w_in, conv_dw_w, conv_dw_b, conv_ln_g, conv_ln_b, conv_w_out) = params
    S = x.shape[1]
    cos, sin = _rope_tables(S)
    q_scale = HEAD_DIM ** -0.5 * math.log2(math.e)
    depth = pre_norm_g.shape[0]
    for i in range(depth):
        j = i // 2
        pre_g = pre_norm_g[i][None, :]
        post_g = post_norm_g[i][None, :]
        if i % 2 == 0:
            q, k, vt, gate = _attn_in(
                x, pre_g, attn_w_in[j].astype(BF16),
                (attn_q_norm_g[j] * q_scale)[None, :], attn_k_norm_g[j][None, :], cos, sin)
            o = _flash(q, k, vt)
            x = _attn_out(o, gate, x, attn_w_out[j].astype(BF16), post_g)
        else:
            dw_w8 = jnp.broadcast_to(conv_dw_w[j][:, None, :], (CONV_KERNEL, SUBLANES, D_MODEL))
            x = _conv_layer(
                x, pre_g, conv_w_in[j].astype(BF16), dw_w8, conv_dw_b[j][None, :],
                conv_ln_g[j][None, :], conv_ln_b[j][None, :], conv_w_out[j].astype(BF16), post_g)
    return x


def kernel(x_prompt, x_sample, pre_norm_g, post_norm_g, attn_w_in, attn_q_norm_g, attn_k_norm_g,
           attn_w_out, conv_w_in, conv_dw_w, conv_dw_b, conv_ln_g, conv_ln_b, conv_w_out):
    params = (pre_norm_g, post_norm_g, attn_w_in, attn_q_norm_g, attn_k_norm_g, attn_w_out,
              conv_w_in, conv_dw_w, conv_dw_b, conv_ln_g, conv_ln_b, conv_w_out)
    return (_trunk(x_prompt, params), _trunk(x_sample, params))
```

```python
import functools
import math

import jax
import jax.numpy as jnp
from jax import lax
from jax.experimental import pallas as pl
from jax.experimental.pallas import tpu as pltpu

D_MODEL = 1024
HEAD_DIM = 128
N_HEADS = 8
N_KV_HEADS = 2
GQA_GROUP = N_HEADS // N_KV_HEADS
ATTN_WIDTH = N_HEADS * HEAD_DIM
KV_WIDTH = N_KV_HEADS * HEAD_DIM
ATTN_IN_WIDTH = 2 * ATTN_WIDTH + 2 * KV_WIDTH
ROPE_AXIS_DIM = HEAD_DIM // 2
ROPE_THETA = 10000.0
CONV_KERNEL = 31
CONV_PAD = CONV_KERNEL // 2
GRID_W = 64
EPS = 1e-6

SUBLANES = 8
LANES = 128

ATTN_IN_ROWS = 1024
KEY_CHUNK = 256
VT_ROWS = HEAD_DIM + 16
Q_ROWS = 512
SUB_ROWS = 256
CHUNK_UNROLL = 16
OUT_ROWS = 1024
CONV_ROWS = 512
CONV_HALO = 16
CONV_ACC_ROWS = 32
CONV_PIECE_COLS = 256

VMEM_LIMIT_BYTES = 56 * 1024 * 1024

F32 = jnp.float32
BF16 = jnp.bfloat16


def _rms_scale(x):
    return lax.rsqrt(jnp.mean(x * x, axis=-1, keepdims=True) + EPS)


def _silu(x):
    return x * jax.nn.sigmoid(x)


def _attn_in_kernel(x_ref, preg_ref, w_ref, gq_ref, gk_ref, cos_ref, sin_ref,
                    q_ref, k_ref, vt_ref, gate_ref):
    R = SUB_ROWS
    lane = lax.broadcasted_iota(jnp.int32, (R, HEAD_DIM), 1)
    first_half = (lane % ROPE_AXIS_DIM) < (ROPE_AXIS_DIM // 2)
    gq = gq_ref[...]
    gk = gk_ref[...]
    for sub in range(x_ref.shape[1] // R):
        rows = slice(sub * R, (sub + 1) * R)
        x = x_ref[0, rows, :]
        h = (x * _rms_scale(x) * preg_ref[...]).astype(BF16)
        proj = jnp.dot(h, w_ref[...], preferred_element_type=F32)
        cos = cos_ref[rows, :]
        sin = sin_ref[rows, :]

        def norm_rope(t, g):
            t = t * _rms_scale(t) * g
            rot = jnp.where(first_half,
                            pltpu.roll(t, HEAD_DIM - ROPE_AXIS_DIM // 2, 1),
                            pltpu.roll(t, ROPE_AXIS_DIM // 2, 1))
            return t * cos + rot * sin

        for hd in range(N_HEADS):
            t = proj[:, hd * HEAD_DIM:(hd + 1) * HEAD_DIM]
            q_ref[0, hd, rows, :] = norm_rope(t, gq).astype(BF16)
        for kv in range(N_KV_HEADS):
            off = ATTN_WIDTH + kv * HEAD_DIM
            k_ref[0, kv, rows, :] = norm_rope(proj[:, off:off + HEAD_DIM], gk).astype(BF16)
            off = ATTN_WIDTH + KV_WIDTH + kv * HEAD_DIM
            chunk, col = divmod(sub * R, KEY_CHUNK)
            vt_ref[0, kv, chunk, 0:HEAD_DIM, col:col + R] = proj[:, off:off + HEAD_DIM].T.astype(BF16)
            vt_ref[0, kv, chunk, HEAD_DIM:VT_ROWS, col:col + R] = jnp.ones((VT_ROWS - HEAD_DIM, R), BF16)
        z = proj[:, ATTN_WIDTH + 2 * KV_WIDTH:]
        gate_ref[0, rows, :] = _silu(z).astype(BF16)


def _attn_in(x, pre_g, w_in, gq, gk, cos, sin):
    B, S, _ = x.shape
    T = ATTN_IN_ROWS
    n_t = S // T
    const = lambda b, i: (0, 0)
    return pl.pallas_call(
        _attn_in_kernel,
        grid=(B, n_t),
        in_specs=[
            pl.BlockSpec((1, T, D_MODEL), lambda b, i: (b, i, 0)),
            pl.BlockSpec((1, D_MODEL), const),
            pl.BlockSpec((D_MODEL, ATTN_IN_WIDTH), const),
            pl.BlockSpec((1, HEAD_DIM), const),
            pl.BlockSpec((1, HEAD_DIM), const),
            pl.BlockSpec((T, HEAD_DIM), lambda b, i: (i, 0)),
            pl.BlockSpec((T, HEAD_DIM), lambda b, i: (i, 0)),
        ],
        out_specs=[
            pl.BlockSpec((1, N_HEADS, T, HEAD_DIM), lambda b, i: (b, 0, i, 0)),
            pl.BlockSpec((1, N_KV_HEADS, T, HEAD_DIM), lambda b, i: (b, 0, i, 0)),
            pl.BlockSpec((1, N_KV_HEADS, T // KEY_CHUNK, VT_ROWS, KEY_CHUNK),
                         lambda b, i: (b, 0, i, 0, 0)),
            pl.BlockSpec((1, T, D_MODEL), lambda b, i: (b, i, 0)),
        ],
        out_shape=[
            jax.ShapeDtypeStruct((B, N_HEADS, S, HEAD_DIM), BF16),
            jax.ShapeDtypeStruct((B, N_KV_HEADS, S, HEAD_DIM), BF16),
            jax.ShapeDtypeStruct((B, N_KV_HEADS, S // KEY_CHUNK, VT_ROWS, KEY_CHUNK), BF16),
            jax.ShapeDtypeStruct((B, S, D_MODEL), BF16),
        ],
        compiler_params=pltpu.CompilerParams(
            dimension_semantics=("parallel", "parallel"),
            vmem_limit_bytes=VMEM_LIMIT_BYTES),
        name="attn_in",
    )(x, pre_g, w_in, gq, gk, cos, sin)


def _flash_kernel(q_ref, k_ref, vt_ref, o_ref, s_scr, acc_scr):
    seq = k_ref.shape[2]
    n_chunks = seq // KEY_CHUNK
    tq = Q_ROWS
    n_q = seq // tq
    nt_dims = (((1,), (1,)), ((), ()))

    def stage(rows_a, head_a, slot_a, unit_b):
        q_a = q_ref[0, head_a, pl.ds(rows_a, tq), :]
        if unit_b is not None:
            rows_b, head_b, slot_b, m_b = unit_b
            acc_scr[...] = jnp.zeros_like(acc_scr)

        def chunk(c, carry):
            m_a = carry
            start = pl.multiple_of(c * KEY_CHUNK, KEY_CHUNK)
            s = lax.dot_general(k_ref[0, 0, pl.ds(start, KEY_CHUNK), :], q_a, nt_dims,
                                preferred_element_type=F32)
            s_scr[slot_a, pl.ds(start, KEY_CHUNK), :] = s
            m_a = jnp.maximum(m_a, jnp.max(s, axis=0, keepdims=True))
            if unit_b is not None:
                p = jnp.exp2(s_scr[slot_b, pl.ds(start, KEY_CHUNK), :] - m_b)
                acc_scr[...] += jnp.dot(vt_ref[0, 0, c], p.astype(BF16),
                                        preferred_element_type=F32)
            return m_a

        m_a = lax.fori_loop(0, n_chunks, chunk, jnp.full((1, tq), -jnp.inf, F32),
                            unroll=CHUNK_UNROLL)
        if unit_b is not None:
            o_t = acc_scr[0:HEAD_DIM, :] * (1.0 / acc_scr[HEAD_DIM:HEAD_DIM + 1, :])
            o_ref[0, pl.ds(rows_b, tq), head_b * HEAD_DIM:(head_b + 1) * HEAD_DIM] = (
                o_t.T.astype(BF16))
        return m_a

    def q_block(qi, m_first):
        rows = pl.multiple_of(qi * tq, tq)
        m = m_first
        for hd in range(1, GQA_GROUP):
            m = stage(rows, hd, hd % 2, (rows, hd - 1, (hd - 1) % 2, m))
        rows_next = pl.multiple_of(jnp.minimum(qi + 1, n_q - 1) * tq, tq)
        last = GQA_GROUP - 1
        return stage(rows_next, 0, 0, (rows, last, last % 2, m))

    m0 = stage(0, 0, 0, None)
    lax.fori_loop(0, n_q, q_block, m0)


def _flash(q, k, vt):
    B, _, S, _ = q.shape
    n_c = S // KEY_CHUNK
    gw = GQA_GROUP * HEAD_DIM
    return pl.pallas_call(
        _flash_kernel,
        grid=(B, N_KV_HEADS),
        in_specs=[
            pl.BlockSpec((1, GQA_GROUP, S, HEAD_DIM), lambda b, g: (b, g, 0, 0)),
            pl.BlockSpec((1, 1, S, HEAD_DIM), lambda b, g: (b, g, 0, 0)),
            pl.BlockSpec((1, 1, n_c, VT_ROWS, KEY_CHUNK), lambda b, g: (b, g, 0, 0, 0)),
        ],
        out_specs=pl.BlockSpec((1, S, gw), lambda b, g: (b, 0, g)),
        out_shape=jax.ShapeDtypeStruct((B, S, ATTN_WIDTH), BF16),
        scratch_shapes=[
            pltpu.VMEM((2, S, Q_ROWS), F32),
            pltpu.VMEM((VT_ROWS, Q_ROWS), F32),
        ],
        compiler_params=pltpu.CompilerParams(
            dimension_semantics=("parallel", "parallel"),
            vmem_limit_bytes=VMEM_LIMIT_BYTES),
        name="flash_attn",
    )(q, k, vt)


def _attn_out_kernel(o_ref, gate_ref, x_ref, w_ref, postg_ref, y_ref):
    R = SUB_ROWS
    for sub in range(x_ref.shape[1] // R):
        rows = slice(sub * R, (sub + 1) * R)
        og = o_ref[0, rows, :] * gate_ref[0, rows, :]
        m = jnp.dot(og, w_ref[...], preferred_element_type=F32)
        y_ref[0, rows, :] = x_ref[0, rows, :] + m * _rms_scale(m) * postg_ref[...]


def _attn_out(o, gate, x, w_out, post_g):
    B, S, _ = x.shape
    T = OUT_ROWS
    const = lambda b, i: (0, 0)
    row = lambda b, i: (b, i, 0)
    return pl.pallas_call(
        _attn_out_kernel,
        grid=(B, S // T),
        in_specs=[
            pl.BlockSpec((1, T, ATTN_WIDTH), row),
            pl.BlockSpec((1, T, ATTN_WIDTH), row),
            pl.BlockSpec((1, T, D_MODEL), row),
            pl.BlockSpec((ATTN_WIDTH, D_MODEL), const),
            pl.BlockSpec((1, D_MODEL), const),
        ],
        out_specs=pl.BlockSpec((1, T, D_MODEL), row),
        out_shape=jax.ShapeDtypeStruct((B, S, D_MODEL), F32),
        compiler_params=pltpu.CompilerParams(
            dimension_semantics=("parallel", "parallel"),
            vmem_limit_bytes=VMEM_LIMIT_BYTES),
        name="attn_out",
    )(o, gate, x, w_out, post_g)


def _conv_layer_kernel(xl_ref, x_ref, xr_ref, preg_ref, win_ref, dww_ref, dwb_ref,
                       lng_ref, lnb_ref, wout_ref, postg_ref, y_ref,
                       u_scr, gate_scr, conv_scr, *, seq_len):
    T = x_ref.shape[1]
    H = CONV_HALO
    TE = T + 2 * H
    C = D_MODEL
    W = CONV_PIECE_COLS
    tiles_per_piece = W // LANES
    i = pl.program_id(1)

    def input_norm():
        xe = jnp.concatenate([xl_ref[0], x_ref[0], xr_ref[0]], axis=0)
        return (xe * _rms_scale(xe) * preg_ref[...]).astype(BF16)

    def project_piece(h, p):
        a = jnp.dot(h, win_ref[:, p * W:(p + 1) * W], preferred_element_type=F32)
        g = jnp.dot(h, win_ref[:, C + p * W:C + (p + 1) * W], preferred_element_type=F32)
        u = a * jax.nn.sigmoid(g)
        t_abs = lax.broadcasted_iota(jnp.int32, (TE, 1), 0) + (i * T - H)
        u = jnp.where((t_abs >= 0) & (t_abs < seq_len), u, 0.0)
        for j in range(tiles_per_piece):
            u_scr[p * tiles_per_piece + j] = u[:, j * LANES:(j + 1) * LANES]
        z = jnp.dot(h[H:H + T], win_ref[:, 2 * C + p * W:2 * C + (p + 1) * W],
                    preferred_element_type=F32)
        gate_scr[:, p * W:(p + 1) * W] = _silu(z)

    def conv_taps(ci):
        first_row = H - CONV_PAD
        groups = CONV_ACC_ROWS // SUBLANES
        lanes = slice(ci * LANES, (ci + 1) * LANES)
        bias = jnp.broadcast_to(dwb_ref[:, lanes], (SUBLANES, LANES))
        for rb in range(0, T, CONV_ACC_ROWS):
            acc = [bias] * groups
            for kk in range(CONV_KERNEL):
                w = dww_ref[kk, :, lanes]
                for gi in range(groups):
                    r0 = rb + gi * SUBLANES + first_row + kk
                    acc[gi] = acc[gi] + u_scr[ci, r0:r0 + SUBLANES, :] * w
            for gi in range(groups):
                r0 = rb + gi * SUBLANES
                conv_scr[r0:r0 + SUBLANES, lanes] = acc[gi]

    def finish():
        c = conv_scr[...]
        mu = jnp.mean(c, axis=-1, keepdims=True)
        cc = c - mu
        var = jnp.mean(cc * cc, axis=-1, keepdims=True)
        ln = cc * lax.rsqrt(var + EPS) * lng_ref[...] + lnb_ref[...]
        y = (_silu(ln) * gate_scr[...]).astype(BF16)
        m = jnp.dot(y, wout_ref[...], preferred_element_type=F32)
        y_ref[0] = x_ref[0] + m * _rms_scale(m) * postg_ref[...]

    h = input_norm()
    for p in range(C // W):
        project_piece(h, p)
    for ci in range(C // LANES):
        conv_taps(ci)
    finish()


def _conv_layer(x, pre_g, w_in, dw_w8, dw_b, ln_g, ln_b, w_out, post_g):
    B, S, _ = x.shape
    T = CONV_ROWS
    H = CONV_HALO
    hb = T // H
    n_hb = S // H
    const2 = lambda b, i: (0, 0)
    const3 = lambda b, i: (0, 0, 0)
    kernel = functools.partial(_conv_layer_kernel, seq_len=S)
    return pl.pallas_call(
        kernel,
        grid=(B, S // T),
        in_specs=[
            pl.BlockSpec((1, H, D_MODEL), lambda b, i: (b, jnp.maximum(i * hb - 1, 0), 0)),
            pl.BlockSpec((1, T, D_MODEL), lambda b, i: (b, i, 0)),
            pl.BlockSpec((1, H, D_MODEL), lambda b, i: (b, jnp.minimum((i + 1) * hb, n_hb - 1), 0)),
            pl.BlockSpec((1, D_MODEL), const2),
            pl.BlockSpec((D_MODEL, 3 * D_MODEL), const2),
            pl.BlockSpec((CONV_KERNEL, SUBLANES, D_MODEL), const3),
            pl.BlockSpec((1, D_MODEL), const2),
            pl.BlockSpec((1, D_MODEL), const2),
            pl.BlockSpec((1, D_MODEL), const2),
            pl.BlockSpec((D_MODEL, D_MODEL), const2),
            pl.BlockSpec((1, D_MODEL), const2),
        ],
        out_specs=pl.BlockSpec((1, T, D_MODEL), lambda b, i: (b, i, 0)),
        out_shape=jax.ShapeDtypeStruct((B, S, D_MODEL), F32),
        scratch_shapes=[
            pltpu.VMEM((D_MODEL // LANES, T + 2 * H, LANES), F32),
            pltpu.VMEM((T, D_MODEL), F32),
            pltpu.VMEM((T, D_MODEL), F32),
        ],
        compiler_params=pltpu.CompilerParams(
            dimension_semantics=("parallel", "parallel"),
            vmem_limit_bytes=VMEM_LIMIT_BYTES),
        name="conv_layer",
    )(x, x, x, pre_g, w_in, dw_w8, dw_b, ln_g, ln_b, w_out, post_g)


def _rope_tables(seq_len):
    rows = seq_len // GRID_W
    row = jnp.repeat(jnp.arange(rows, dtype=F32), GRID_W)
    col = jnp.tile(jnp.arange(GRID_W, dtype=F32), rows)
    inv_freq = ROPE_THETA ** (-jnp.arange(0, ROPE_AXIS_DIM, 2, dtype=F32) / ROPE_AXIS_DIM)
    ang_r = row[:, None] * inv_freq[None, :]
    ang_c = col[:, None] * inv_freq[None, :]
    ang = jnp.concatenate([ang_r, ang_r, ang_c, ang_c], axis=-1)
    half = ROPE_AXIS_DIM // 2
    sign = jnp.where((jnp.arange(HEAD_DIM) % ROPE_AXIS_DIM) < half, -1.0, 1.0).astype(F32)
    return jnp.cos(ang), jnp.sin(ang) * sign[None, :]


def _trunk(x, params):
    (pre_norm_g, post_norm_g, attn_w_in, attn_q_norm_g, attn_k_norm_g, attn_w_out,
     conv_w_in, conv_dw_w, conv_dw_b, conv_ln_g, conv_ln_b, conv_w_out) = params
    S = x.shape[1]
    cos, sin = _rope_tables(S)
    q_scale = HEAD_DIM ** -0.5 * math.log2(math.e)
    depth = pre_norm_g.shape[0]
    for i in range(depth):
        j = i // 2
        pre_g = pre_norm_g[i][None, :]
        post_g = post_norm_g[i][None, :]
        if i % 2 == 0:
            q, k, vt, gate = _attn_in(
                x, pre_g, attn_w_in[j].astype(BF16),
                (attn_q_norm_g[j] * q_scale)[None, :], attn_k_norm_g[j][None, :], cos, sin)
            o = _flash(q, k, vt)
            x = _attn_out(o, gate, x, attn_w_out[j].astype(BF16), post_g)
        else:
            dw_w8 = jnp.broadcast_to(conv_dw_w[j][:, None, :], (CONV_KERNEL, SUBLANES, D_MODEL))
            x = _conv_layer(
                x, pre_g, conv_w_in[j].astype(BF16), dw_w8, conv_dw_b[j][None, :],
                conv_ln_g[j][None, :], conv_ln_b[j][None, :], conv_w_out[j].astype(BF16), post_g)
    return x


def kernel(x_prompt, x_sample, pre_norm_g, post_norm_g, attn_w_in, attn_q_norm_g, attn_k_norm_g,
           attn_w_out, conv_w_in, conv_dw_w, conv_dw_b, conv_ln_g, conv_ln_b, conv_w_out):
    params = (pre_norm_g, post_norm_g, attn_w_in, attn_q_norm_g, attn_k_norm_g, attn_w_out,
              conv_w_in, conv_dw_w, conv_dw_b, conv_ln_g, conv_ln_b, conv_w_out)
    return (_trunk(x_prompt, params), _trunk(x_sample, params))
```

```python
import functools
import math

import jax
import jax.numpy as jnp
from jax import lax
from jax.experimental import pallas as pl
from jax.experimental.pallas import tpu as pltpu

D_MODEL = 1024
HEAD_DIM = 128
N_HEADS = 8
N_KV_HEADS = 2
GQA_GROUP = N_HEADS // N_KV_HEADS
ATTN_WIDTH = N_HEADS * HEAD_DIM
KV_WIDTH = N_KV_HEADS * HEAD_DIM
ATTN_IN_WIDTH = 2 * ATTN_WIDTH + 2 * KV_WIDTH
ROPE_AXIS_DIM = HEAD_DIM // 2
ROPE_THETA = 10000.0
CONV_KERNEL = 31
CONV_PAD = CONV_KERNEL // 2
GRID_W = 64
EPS = 1e-6

SUBLANES = 8
LANES = 128

ATTN_IN_ROWS = 1024
KEY_CHUNK = 512
VT_ROWS = HEAD_DIM + 16
Q_ROWS = 512
SUB_ROWS = 256
CHUNK_UNROLL = 8
OUT_ROWS = 1024
RING_SLOTS = 3
CONV_ROWS = 512
CONV_HALO = 16
CONV_ACC_ROWS = 32
CONV_PIECE_COLS = 256

VMEM_LIMIT_BYTES = 56 * 1024 * 1024

F32 = jnp.float32
BF16 = jnp.bfloat16


def _rms_scale(x):
    return lax.rsqrt(jnp.mean(x * x, axis=-1, keepdims=True) + EPS)


def _silu(x):
    return x * jax.nn.sigmoid(x)


def _attn_in_kernel(x_ref, preg_ref, w_ref, gq_ref, gk_ref, cos_ref, sin_ref,
                    q_ref, k_ref, vt_ref, gate_ref):
    R = SUB_ROWS
    lane = lax.broadcasted_iota(jnp.int32, (R, HEAD_DIM), 1)
    first_half = (lane % ROPE_AXIS_DIM) < (ROPE_AXIS_DIM // 2)
    gq = gq_ref[...]
    gk = gk_ref[...]
    for sub in range(x_ref.shape[1] // R):
        rows = slice(sub * R, (sub + 1) * R)
        x = x_ref[0, rows, :]
        h = (x * _rms_scale(x) * preg_ref[...]).astype(BF16)
        proj = jnp.dot(h, w_ref[...], preferred_element_type=F32)
        cos = cos_ref[rows, :]
        sin = sin_ref[rows, :]

        def norm_rope(t, g):
            t = t * _rms_scale(t) * g
            rot = jnp.where(first_half,
                            pltpu.roll(t, HEAD_DIM - ROPE_AXIS_DIM // 2, 1),
                            pltpu.roll(t, ROPE_AXIS_DIM // 2, 1))
            return t * cos + rot * sin

        for hd in range(N_HEADS):
            t = proj[:, hd * HEAD_DIM:(hd + 1) * HEAD_DIM]
            q_ref[0, hd, rows, :] = norm_rope(t, gq).astype(BF16)
        for kv in range(N_KV_HEADS):
            off = ATTN_WIDTH + kv * HEAD_DIM
            k_ref[0, kv, rows, :] = norm_rope(proj[:, off:off + HEAD_DIM], gk).astype(BF16)
            off = ATTN_WIDTH + KV_WIDTH + kv * HEAD_DIM
            chunk, col = divmod(sub * R, KEY_CHUNK)
            vt_ref[0, kv, chunk, 0:HEAD_DIM, col:col + R] = proj[:, off:off + HEAD_DIM].T.astype(BF16)
            vt_ref[0, kv, chunk, HEAD_DIM:VT_ROWS, col:col + R] = jnp.ones((VT_ROWS - HEAD_DIM, R), BF16)
        z = proj[:, ATTN_WIDTH + 2 * KV_WIDTH:]
        gate_ref[0, rows, :] = _silu(z).astype(BF16)


def _attn_in(x, pre_g, w_in, gq, gk, cos, sin):
    B, S, _ = x.shape
    T = ATTN_IN_ROWS
    n_t = S // T
    const = lambda b, i: (0, 0)
    return pl.pallas_call(
        _attn_in_kernel,
        grid=(B, n_t),
        in_specs=[
            pl.BlockSpec((1, T, D_MODEL), lambda b, i: (b, i, 0)),
            pl.BlockSpec((1, D_MODEL), const),
            pl.BlockSpec((D_MODEL, ATTN_IN_WIDTH), const),
            pl.BlockSpec((1, HEAD_DIM), const),
            pl.BlockSpec((1, HEAD_DIM), const),
            pl.BlockSpec((T, HEAD_DIM), lambda b, i: (i, 0)),
            pl.BlockSpec((T, HEAD_DIM), lambda b, i: (i, 0)),
        ],
        out_specs=[
            pl.BlockSpec((1, N_HEADS, T, HEAD_DIM), lambda b, i: (b, 0, i, 0)),
            pl.BlockSpec((1, N_KV_HEADS, T, HEAD_DIM), lambda b, i: (b, 0, i, 0)),
            pl.BlockSpec((1, N_KV_HEADS, T // KEY_CHUNK, VT_ROWS, KEY_CHUNK),
                         lambda b, i: (b, 0, i, 0, 0)),
            pl.BlockSpec((1, T, D_MODEL), lambda b, i: (b, i, 0)),
        ],
        out_shape=[
            jax.ShapeDtypeStruct((B, N_HEADS, S, HEAD_DIM), BF16),
            jax.ShapeDtypeStruct((B, N_KV_HEADS, S, HEAD_DIM), BF16),
            jax.ShapeDtypeStruct((B, N_KV_HEADS, S // KEY_CHUNK, VT_ROWS, KEY_CHUNK), BF16),
            jax.ShapeDtypeStruct((B, S, D_MODEL), BF16),
        ],
        compiler_params=pltpu.CompilerParams(
            dimension_semantics=("parallel", "parallel"),
            vmem_limit_bytes=VMEM_LIMIT_BYTES),
        name="attn_in",
    )(x, pre_g, w_in, gq, gk, cos, sin)


def _flash_kernel(q_ref, k_ref, vt_ref, o_ref, s_scr, acc_scr):
    seq = k_ref.shape[2]
    n_chunks = seq // KEY_CHUNK
    tq = Q_ROWS
    n_q = seq // tq
    nt_dims = (((1,), (1,)), ((), ()))

    def stage(rows_a, head_a, slot_a, unit_b):
        q_a = q_ref[0, head_a, pl.ds(rows_a, tq), :]
        if unit_b is not None:
            rows_b, head_b, slot_b, m_b = unit_b
            acc_scr[...] = jnp.zeros_like(acc_scr)

        def chunk(c, carry):
            m_a = carry
            start = pl.multiple_of(c * KEY_CHUNK, KEY_CHUNK)
            s = lax.dot_general(k_ref[0, 0, pl.ds(start, KEY_CHUNK), :], q_a, nt_dims,
                                preferred_element_type=F32)
            s_scr[slot_a, pl.ds(start, KEY_CHUNK), :] = s
            m_a = jnp.maximum(m_a, jnp.max(s, axis=0, keepdims=True))
            if unit_b is not None:
                p = jnp.exp2(s_scr[slot_b, pl.ds(start, KEY_CHUNK), :] - m_b)
                acc_scr[...] += jnp.dot(vt_ref[0, 0, c], p.astype(BF16),
                                        preferred_element_type=F32)
            return m_a

        m_a = lax.fori_loop(0, n_chunks, chunk, jnp.full((1, tq), -jnp.inf, F32),
                            unroll=CHUNK_UNROLL)
        if unit_b is not None:
            o_t = acc_scr[0:HEAD_DIM, :] * (1.0 / acc_scr[HEAD_DIM:HEAD_DIM + 1, :])
            o_ref[0, pl.ds(rows_b, tq), head_b * HEAD_DIM:(head_b + 1) * HEAD_DIM] = (
                o_t.T.astype(BF16))
        return m_a

    def q_block(qi, m_first):
        rows = pl.multiple_of(qi * tq, tq)
        m = m_first
        for hd in range(1, GQA_GROUP):
            m = stage(rows, hd, hd % 2, (rows, hd - 1, (hd - 1) % 2, m))
        rows_next = pl.multiple_of(jnp.minimum(qi + 1, n_q - 1) * tq, tq)
        last = GQA_GROUP - 1
        return stage(rows_next, 0, 0, (rows, last, last % 2, m))

    m0 = stage(0, 0, 0, None)
    lax.fori_loop(0, n_q, q_block, m0)


def _flash(q, k, vt):
    B, _, S, _ = q.shape
    n_c = S // KEY_CHUNK
    gw = GQA_GROUP * HEAD_DIM
    return pl.pallas_call(
        _flash_kernel,
        grid=(B, N_KV_HEADS),
        in_specs=[
            pl.BlockSpec((1, GQA_GROUP, S, HEAD_DIM), lambda b, g: (b, g, 0, 0)),
            pl.BlockSpec((1, 1, S, HEAD_DIM), lambda b, g: (b, g, 0, 0)),
            pl.BlockSpec((1, 1, n_c, VT_ROWS, KEY_CHUNK), lambda b, g: (b, g, 0, 0, 0)),
        ],
        out_specs=pl.BlockSpec((1, S, gw), lambda b, g: (b, 0, g)),
        out_shape=jax.ShapeDtypeStruct((B, S, ATTN_WIDTH), BF16),
        scratch_shapes=[
            pltpu.VMEM((2, S, Q_ROWS), F32),
            pltpu.VMEM((VT_ROWS, Q_ROWS), F32),
        ],
        compiler_params=pltpu.CompilerParams(
            dimension_semantics=("parallel", "parallel"),
            vmem_limit_bytes=VMEM_LIMIT_BYTES),
        name="flash_attn",
    )(q, k, vt)


def _attn_out_kernel(o_hbm, gate_hbm, x_hbm, w_ref, postg_ref, y_ref,
                     o_buf, gate_buf, x_buf, sems):
    T = y_ref.shape[1]
    n_t = pl.num_programs(1)
    n_steps = pl.num_programs(0) * n_t
    step = pl.program_id(0) * n_t + pl.program_id(1)

    def tile_copies(s):
        b = s // n_t
        rows = pl.ds(pl.multiple_of((s % n_t) * T, T), T)
        slot = s % RING_SLOTS
        return [pltpu.make_async_copy(src.at[b, rows, :], dst.at[slot], sems.at[k, slot])
                for k, (src, dst) in enumerate(((o_hbm, o_buf), (gate_hbm, gate_buf),
                                                (x_hbm, x_buf)))]

    @pl.when(step == 0)
    def _():
        for s in range(RING_SLOTS - 1):
            for cp in tile_copies(s):
                cp.start()

    @pl.when(step + RING_SLOTS - 1 < n_steps)
    def _():
        for cp in tile_copies(step + RING_SLOTS - 1):
            cp.start()

    for cp in tile_copies(step):
        cp.wait()

    slot = step % RING_SLOTS
    R = SUB_ROWS
    for sub in range(T // R):
        rows = slice(sub * R, (sub + 1) * R)
        og = o_buf[slot, rows, :] * gate_buf[slot, rows, :]
        m = jnp.dot(og, w_ref[...], preferred_element_type=F32)
        y_ref[0, rows, :] = x_buf[slot, rows, :] + m * _rms_scale(m) * postg_ref[...]


def _attn_out(o, gate, x, w_out, post_g):
    B, S, _ = x.shape
    T = OUT_ROWS
    assert B * (S // T) >= RING_SLOTS - 1
    const = lambda b, i: (0, 0)
    hbm = pl.BlockSpec(memory_space=pl.ANY)
    return pl.pallas_call(
        _attn_out_kernel,
        grid=(B, S // T),
        in_specs=[
            hbm, hbm, hbm,
            pl.BlockSpec((ATTN_WIDTH, D_MODEL), const),
            pl.BlockSpec((1, D_MODEL), const),
        ],
        out_specs=pl.BlockSpec((1, T, D_MODEL), lambda b, i: (b, i, 0)),
        out_shape=jax.ShapeDtypeStruct((B, S, D_MODEL), F32),
        scratch_shapes=[
            pltpu.VMEM((RING_SLOTS, T, ATTN_WIDTH), BF16),
            pltpu.VMEM((RING_SLOTS, T, ATTN_WIDTH), BF16),
            pltpu.VMEM((RING_SLOTS, T, D_MODEL), F32),
            pltpu.SemaphoreType.DMA((3, RING_SLOTS)),
        ],
        compiler_params=pltpu.CompilerParams(
            dimension_semantics=("arbitrary", "arbitrary"),
            vmem_limit_bytes=VMEM_LIMIT_BYTES),
        name="attn_out",
    )(o, gate, x, w_out, post_g)


def _conv_layer_kernel(xl_ref, x_ref, xr_ref, preg_ref, win_ref, dww_ref, dwb_ref,
                       lng_ref, lnb_ref, wout_ref, postg_ref, y_ref,
                       u_scr, gate_scr, conv_scr, *, seq_len):
    T = x_ref.shape[1]
    H = CONV_HALO
    TE = T + 2 * H
    C = D_MODEL
    W = CONV_PIECE_COLS
    tiles_per_piece = W // LANES
    i = pl.program_id(1)

    def input_norm():
        xe = jnp.concatenate([xl_ref[0], x_ref[0], xr_ref[0]], axis=0)
        return (xe * _rms_scale(xe) * preg_ref[...]).astype(BF16)

    def project_piece(h, p):
        a = jnp.dot(h, win_ref[:, p * W:(p + 1) * W], preferred_element_type=F32)
        g = jnp.dot(h, win_ref[:, C + p * W:C + (p + 1) * W], preferred_element_type=F32)
        u = a * jax.nn.sigmoid(g)
        t_abs = lax.broadcasted_iota(jnp.int32, (TE, 1), 0) + (i * T - H)
        u = jnp.where((t_abs >= 0) & (t_abs < seq_len), u, 0.0)
        for j in range(tiles_per_piece):
            u_scr[p * tiles_per_piece + j] = u[:, j * LANES:(j + 1) * LANES]
        z = jnp.dot(h[H:H + T], win_ref[:, 2 * C + p * W:2 * C + (p + 1) * W],
                    preferred_element_type=F32)
        gate_scr[:, p * W:(p + 1) * W] = _silu(z)

    def conv_taps(ci):
        first_row = H - CONV_PAD
        groups = CONV_ACC_ROWS // SUBLANES
        lanes = slice(ci * LANES, (ci + 1) * LANES)
        bias = jnp.broadcast_to(dwb_ref[:, lanes], (SUBLANES, LANES))
        for rb in range(0, T, CONV_ACC_ROWS):
            acc = [bias] * groups
            for kk in range(CONV_KERNEL):
                w = dww_ref[kk, :, lanes]
                for gi in range(groups):
                    r0 = rb + gi * SUBLANES + first_row + kk
                    acc[gi] = acc[gi] + u_scr[ci, r0:r0 + SUBLANES, :] * w
            for gi in range(groups):
                r0 = rb + gi * SUBLANES
                conv_scr[r0:r0 + SUBLANES, lanes] = acc[gi]

    def finish():
        c = conv_scr[...]
        mu = jnp.mean(c, axis=-1, keepdims=True)
        cc = c - mu
        var = jnp.mean(cc * cc, axis=-1, keepdims=True)
        ln = cc * lax.rsqrt(var + EPS) * lng_ref[...] + lnb_ref[...]
        y = (_silu(ln) * gate_scr[...]).astype(BF16)
        m = jnp.dot(y, wout_ref[...], preferred_element_type=F32)
        y_ref[0] = x_ref[0] + m * _rms_scale(m) * postg_ref[...]

    h = input_norm()
    for p in range(C // W):
        project_piece(h, p)
    for ci in range(C // LANES):
        conv_taps(ci)
    finish()


def _conv_layer(x, pre_g, w_in, dw_w8, dw_b, ln_g, ln_b, w_out, post_g):
    B, S, _ = x.shape
    T = CONV_ROWS
    H = CONV_HALO
    hb = T // H
    n_hb = S // H
    const2 = lambda b, i: (0, 0)
    const3 = lambda b, i: (0, 0, 0)
    kernel = functools.partial(_conv_layer_kernel, seq_len=S)
    return pl.pallas_call(
        kernel,
        grid=(B, S // T),
        in_specs=[
            pl.BlockSpec((1, H, D_MODEL), lambda b, i: (b, jnp.maximum(i * hb - 1, 0), 0)),
            pl.BlockSpec((1, T, D_MODEL), lambda b, i: (b, i, 0)),
            pl.BlockSpec((1, H, D_MODEL), lambda b, i: (b, jnp.minimum((i + 1) * hb, n_hb - 1), 0)),
            pl.BlockSpec((1, D_MODEL), const2),
            pl.BlockSpec((D_MODEL, 3 * D_MODEL), const2),
            pl.BlockSpec((CONV_KERNEL, SUBLANES, D_MODEL), const3),
            pl.BlockSpec((1, D_MODEL), const2),
            pl.BlockSpec((1, D_MODEL), const2),
            pl.BlockSpec((1, D_MODEL), const2),
            pl.BlockSpec((D_MODEL, D_MODEL), const2),
            pl.BlockSpec((1, D_MODEL), const2),
        ],
        out_specs=pl.BlockSpec((1, T, D_MODEL), lambda b, i: (b, i, 0)),
        out_shape=jax.ShapeDtypeStruct((B, S, D_MODEL), F32),
        scratch_shapes=[
            pltpu.VMEM((D_MODEL // LANES, T + 2 * H, LANES), F32),
            pltpu.VMEM((T, D_MODEL), F32),
            pltpu.VMEM((T, D_MODEL), F32),
        ],
        compiler_params=pltpu.CompilerParams(
            dimension_semantics=("parallel", "parallel"),
            vmem_limit_bytes=VMEM_LIMIT_BYTES),
        name="conv_layer",
    )(x, x, x, pre_g, w_in, dw_w8, dw_b, ln_g, ln_b, w_out, post_g)


def _rope_tables(seq_len):
    rows = seq_len // GRID_W
    row = jnp.repeat(jnp.arange(rows, dtype=F32), GRID_W)
    col = jnp.tile(jnp.arange(GRID_W, dtype=F32), rows)
    inv_freq = ROPE_THETA ** (-jnp.arange(0, ROPE_AXIS_DIM, 2, dtype=F32) / ROPE_AXIS_DIM)
    ang_r = row[:, None] * inv_freq[None, :]
    ang_c = col[:, None] * inv_freq[None, :]
    ang = jnp.concatenate([ang_r, ang_r, ang_c, ang_c], axis=-1)
    half = ROPE_AXIS_DIM // 2
    sign = jnp.where((jnp.arange(HEAD_DIM) % ROPE_AXIS_DIM) < half, -1.0, 1.0).astype(F32)
    return jnp.cos(ang), jnp.sin(ang) * sign[None, :]


def _trunk(x, params):
    (pre_norm_g, post_norm_g, attn_w_in, attn_q_norm_g, attn_k_norm_g, attn_w_out,
     conv_w_in, conv_dw_w, conv_dw_b, conv_ln_g, conv_ln_b, conv_w_out) = params
    S = x.shape[1]
    cos, sin = _rope_tables(S)
    q_scale = HEAD_DIM ** -0.5 * math.log2(math.e)
    depth = pre_norm_g.shape[0]
    for i in range(depth):
        j = i // 2
        pre_g = pre_norm_g[i][None, :]
        post_g = post_norm_g[i][None, :]
        if i % 2 == 0:
            q, k, vt, gate = _attn_in(
                x, pre_g, attn_w_in[j].astype(BF16),
                (attn_q_norm_g[j] * q_scale)[None, :], attn_k_norm_g[j][None, :], cos, sin)
            o = _flash(q, k, vt)
            x = _attn_out(o, gate, x, attn_w_out[j].astype(BF16), post_g)
        else:
            dw_w8 = jnp.broadcast_to(conv_dw_w[j][:, None, :], (CONV_KERNEL, SUBLANES, D_MODEL))
            x = _conv_layer(
                x, pre_g, conv_w_in[j].astype(BF16), dw_w8, conv_dw_b[j][None, :],
                conv_ln_g[j][None, :], conv_ln_b[j][None, :], conv_w_out[j].astype(BF16), post_g)
    return x


def kernel(x_prompt, x_sample, pre_norm_g, post_norm_g, attn_w_in, attn_q_norm_g, attn_k_norm_g,
           attn_w_out, conv_w_in, conv_dw_w, conv_dw_b, conv_ln_g, conv_ln_b, conv_w_out):
    params = (pre_norm_g, post_norm_g, attn_w_in, attn_q_norm_g, attn_k_norm_g, attn_w_out,
              conv_w_in, conv_dw_w, conv_dw_b, conv_ln_g, conv_ln_b, conv_w_out)
    return (_trunk(x_prompt, params), _trunk(x_sample, params))
```
